```python
import math
import jax, jax.numpy as jnp
from jax import lax
import numpy as np

D_MODEL = 1024
BATCH = 2
SEQ = 8192
DEPTH = 2
DEC_BATCH = 4
DEC_SEQ = 8192
PAST_LEN = 128

HEAD_DIM = 64
W_CONV = D_MODEL // 4
W_SCONV = D_MODEL // 4
N_DIL_HEADS = D_MODEL // 4 // HEAD_DIM
N_Q_HEADS = D_MODEL // 4 // HEAD_DIM
N_KV_HEADS = N_Q_HEADS // 2
W_DIL = N_DIL_HEADS * HEAD_DIM
W_GQA = N_Q_HEADS * HEAD_DIM
D_MIX = W_CONV + W_SCONV + W_DIL + W_GQA
CONV_A_WIDTH = 31
CONV_B_WIDTH = 3
DIL_PATTERNS = ((128, 1), (512, 4), (2048, 16))
Q_BLOCK = 128
GRID_W = 64
ROPE_THETA = 10000.0
N_GROUPS = 4
EXPERTS_PER_GROUP = 4
N_EXPERTS = N_GROUPS * EXPERTS_PER_GROUP
D_EXPERT = D_MODEL // 2
TOP_K_INNER = 2
EPS = 1e-6
NEG_INF = -1e30

SZ_A = 2 * W_CONV
SZ_B = 3 * W_SCONV
SZ_C = 3 * W_DIL
SZ_D = (N_Q_HEADS + 2 * N_KV_HEADS) * HEAD_DIM
D_IN = SZ_A + SZ_B + SZ_C + SZ_D
SPLITS_IN = (SZ_A, SZ_A + SZ_B, SZ_A + SZ_B + SZ_C)

kernel_name = "hybrid_parallel_group_encoder"


def rms_norm(x, g):
    x32 = x.astype(jnp.float32)
    y = x32 * lax.rsqrt(jnp.mean(x32 * x32, axis=-1, keepdims=True) + EPS)
    return (y * g.astype(jnp.float32)).astype(x.dtype)


def layer_norm(x, g, b):
    x32 = x.astype(jnp.float32)
    mu = jnp.mean(x32, axis=-1, keepdims=True)
    xc = x32 - mu
    var = jnp.mean(xc * xc, axis=-1, keepdims=True)
    return (xc * lax.rsqrt(var + EPS) * g.astype(jnp.float32) + b.astype(jnp.float32)).astype(x.dtype)


def depthwise_conv(x, w):
    k = w.shape[0]
    return lax.conv_general_dilated(
        x, w[:, None, :].astype(x.dtype), window_strides=(1,),
        padding=[(k // 2, k // 2)], dimension_numbers=("NWC", "WIO", "NWC"),
        feature_group_count=x.shape[-1])


def rope_angles(pos, dim):
    inv = ROPE_THETA ** (-jnp.arange(0, dim, 2, dtype=jnp.float32) / dim)
    return pos.astype(jnp.float32)[:, None] * inv[None, :]


def apply_rope(x, ang):
    x32 = x.astype(jnp.float32)
    x1, x2 = jnp.split(x32, 2, axis=-1)
    cos = jnp.cos(ang)[None, :, None, :]
    sin = jnp.sin(ang)[None, :, None, :]
    return jnp.concatenate([x1 * cos - x2 * sin, x2 * cos + x1 * sin], axis=-1).astype(x.dtype)


def dilated_band_attention(q, k, v, dil, half):
    b, s, h, hd = q.shape
    L = s // dil

    def to_sub(t):
        return t.reshape(b, L, dil, h, hd).transpose(0, 2, 1, 3, 4).reshape(b * dil, L, h, hd)

    qs, ks, vs = to_sub(q), to_sub(k), to_sub(v)
    nb = -(-L // Q_BLOCK)
    lp = nb * Q_BLOCK
    span = Q_BLOCK + 2 * half
    qs = jnp.pad(qs, ((0, 0), (0, lp - L), (0, 0), (0, 0))).reshape(b * dil, nb, Q_BLOCK, h, hd)
    pad_kv = ((0, 0), (half, lp - L + half), (0, 0), (0, 0))
    idx = jnp.arange(nb)[:, None] * Q_BLOCK + jnp.arange(span)[None, :]
    kb = jnp.pad(ks, pad_kv)[:, idx]
    vb = jnp.pad(vs, pad_kv)[:, idx]
    qi = jnp.arange(Q_BLOCK)[:, None]
    kj = jnp.arange(span)[None, :]
    key_pos = idx - half
    band = (kj - qi >= 0) & (kj - qi <= 2 * half)
    valid = (key_pos >= 0) & (key_pos < L)
    mask = (band[None] & valid[:, None, :])[None, :, None]
    sc = jnp.einsum("nbqhd,nbkhd->nbhqk", qs, kb, preferred_element_type=jnp.float32)
    sc = jnp.where(mask, sc, NEG_INF)
    m = jnp.max(sc, axis=-1)
    p = jnp.where(mask, jnp.exp(sc - m[..., None]), 0.0)
    l = jnp.sum(p, axis=-1)
    num = jnp.einsum("nbhqk,nbkhd->nbqhd", p, vb.astype(jnp.float32))

    def back(t):
        t = t[:, :L].reshape((b, dil, L) + t.shape[2:])
        t = jnp.swapaxes(t, 1, 2)
        return t.reshape((b, s) + t.shape[3:])

    num = back(num.reshape(b * dil, lp, h, hd))
    l = back(l.transpose(0, 1, 3, 2).reshape(b * dil, lp, h))
    m = back(m.transpose(0, 1, 3, 2).reshape(b * dil, lp, h))
    return num, l, m


def dilated_mixture_attention(q, k, v):
    outs = [dilated_band_attention(q, k, v, dil, win // (2 * dil)) for (win, dil) in DIL_PATTERNS]
    nums = jnp.stack([o[0] for o in outs])
    ls = jnp.stack([o[1] for o in outs])
    ms = jnp.stack([o[2] for o in outs])
    wts = jnp.exp(ms - jnp.max(ms, axis=0, keepdims=True))
    out = jnp.sum(wts[..., None] * nums, axis=0) / jnp.sum(wts * ls, axis=0)[..., None]
    return out.astype(v.dtype)


def gqa_block_attention(q, k, v):
    b, s, hq, hd = q.shape
    g = hq // N_KV_HEADS
    nb = s // Q_BLOCK
    qb = q.reshape(b, nb, Q_BLOCK, N_KV_HEADS, g, hd).transpose(1, 0, 2, 3, 4, 5)

    def one_block(qblk):
        sc = jnp.einsum("bqkgd,bskd->bkgqs", qblk, k, preferred_element_type=jnp.float32)
        p = jax.nn.softmax(sc, axis=-1)
        return jnp.einsum("bkgqs,bskd->bqkgd", p.astype(v.dtype), v)

    o = lax.map(one_block, qb)
    return o.transpose(1, 0, 2, 3, 4, 5).reshape(b, s, hq * hd)


def hierarchical_moe(x, w_group, b_group, w_router, b_router, w_gate, w_up, w_down):
    b, s, d = x.shape
    xt = x.reshape(b * s, d)
    g_logits = jnp.matmul(xt, w_group).astype(jnp.float32) + b_group.astype(jnp.float32)
    g_w, g_idx = lax.top_k(jax.nn.softmax(g_logits, axis=-1), 1)
    e_logits = (jnp.matmul(xt, w_router).astype(jnp.float32) + b_router.astype(jnp.float32))
    e_logits = e_logits.reshape(-1, N_GROUPS, EXPERTS_PER_GROUP)
    in_group = jnp.take_along_axis(e_logits, g_idx[:, :, None], axis=1)[:, 0]
    top_l, top_i = lax.top_k(in_group, TOP_K_INNER)
    inner_w = jax.nn.softmax(top_l, axis=-1) * g_w
    expert_id = g_idx * EXPERTS_PER_GROUP + top_i
    gates = jnp.einsum("nk,nke->ne", inner_w, jax.nn.one_hot(expert_id, N_EXPERTS, dtype=jnp.float32))
    y = jnp.zeros(xt.shape, jnp.float32)
    for e in range(N_EXPERTS):
        hdn = jax.nn.silu(jnp.matmul(xt, w_gate[e])) * jnp.matmul(xt, w_up[e])
        y = y + gates[:, e:e + 1] * jnp.matmul(hdn, w_down[e]).astype(jnp.float32)
    return y.astype(x.dtype).reshape(b, s, d)


def encoder_layer(x, norm1_g, w_in, conv_a_w, conv_a_b, ln_a_g, ln_a_b, conv_b_w,
                  qn_c, kn_c, qn_d, kn_d, w_out, norm2_g, w_group, b_group,
                  w_router, b_router, w_gate, w_up, w_down):
    b, s, _ = x.shape
    rows = s // GRID_W
    pos = jnp.arange(s)
    row = jnp.repeat(jnp.arange(rows), GRID_W)
    col = jnp.tile(jnp.arange(GRID_W), rows)
    scale = HEAD_DIM ** -0.5

    h = rms_norm(x, norm1_g)
    proj = jnp.einsum("bsd,de->bse", h, w_in)
    pa, pb, pc, pd = jnp.split(proj, SPLITS_IN, axis=-1)

    va, ga = jnp.split(pa, 2, axis=-1)
    ua = depthwise_conv(va * jax.nn.sigmoid(ga), conv_a_w) + conv_a_b.astype(x.dtype)
    ua = jax.nn.silu(layer_norm(ua, ln_a_g, ln_a_b))

    gb, gc, hb = jnp.split(pb, 3, axis=-1)
    ub = gb * depthwise_conv(gc * hb, conv_b_w)

    qc, kc, vc = [t.reshape(b, s, N_DIL_HEADS, HEAD_DIM) for t in jnp.split(pc, 3, axis=-1)]
    ang = rope_angles(pos, HEAD_DIM)
    qc = apply_rope(rms_norm(qc, qn_c), ang) * scale
    kc = apply_rope(rms_norm(kc, kn_c), ang)
    uc = dilated_mixture_attention(qc, kc, vc).reshape(b, s, W_DIL)

    kv_w = N_KV_HEADS * HEAD_DIM
    qd = pd[..., :W_GQA].reshape(b, s, N_Q_HEADS, HEAD_DIM)
    kd = pd[..., W_GQA:W_GQA + kv_w].reshape(b, s, N_KV_HEADS, HEAD_DIM)
    vd = pd[..., W_GQA + kv_w:].reshape(b, s, N_KV_HEADS, HEAD_DIM)
    half = HEAD_DIM // 2
    ang_r = rope_angles(row, half)
    ang_c = rope_angles(col, half)

    def axial(t):
        return jnp.concatenate([apply_rope(t[..., :half], ang_r), apply_rope(t[..., half:], ang_c)], axis=-1)

    qd = axial(rms_norm(qd, qn_d)) * scale
    kd = axial(rms_norm(kd, kn_d))
    ud = gqa_block_attention(qd, kd, vd)

    mix = jnp.concatenate([ua, ub, uc, ud], axis=-1)
    x = x + jnp.einsum("bse,ed->bsd", mix, w_out)
    x = x + hierarchical_moe(rms_norm(x, norm2_g), w_group, b_group, w_router, b_router, w_gate, w_up, w_down)
    return x


def trunk(x, norm1_g, w_in, conv_a_w, conv_a_b, ln_a_g, ln_a_b, conv_b_w,
          qn_c, kn_c, qn_d, kn_d, w_out, norm2_g, w_group, b_group,
          w_router, b_router, w_gate, w_up, w_down):
    for l in range(DEPTH):
        x = encoder_layer(x, norm1_g[l], w_in[l], conv_a_w[l], conv_a_b[l], ln_a_g[l], ln_a_b[l],
                          conv_b_w[l], qn_c[l], kn_c[l], qn_d[l], kn_d[l], w_out[l], norm2_g[l],
                          w_group[l], b_group[l], w_router[l], b_router[l], w_gate[l], w_up[l], w_down[l])
    return x


def setup_inputs(seed: int = 0) -> dict:
    key = jax.random.key(seed)
    ks = jax.random.split(key, 24)
    f32 = jnp.float32

    def nrm(k, shape, sc):
        return jax.random.normal(k, shape, f32) * sc

    def gain(k, shape):
        return 1.0 + 0.02 * jax.random.normal(k, shape, f32)

    return {
        "x_prompt": nrm(ks[0], (BATCH, SEQ, D_MODEL), 1.0),
        "x_sample": nrm(ks[1], (DEC_BATCH, DEC_SEQ, D_MODEL), 1.0),
        "norm1_g": gain(ks[2], (DEPTH, D_MODEL)),
        "w_in": nrm(ks[3], (DEPTH, D_MODEL, D_IN), D_MODEL ** -0.5),
        "conv_a_w": nrm(ks[4], (DEPTH, CONV_A_WIDTH, W_CONV), CONV_A_WIDTH ** -0.5),
        "conv_a_b": nrm(ks[5], (DEPTH, W_CONV), 0.02),
        "ln_a_g": gain(ks[6], (DEPTH, W_CONV)),
        "ln_a_b": nrm(ks[7], (DEPTH, W_CONV), 0.02),
        "conv_b_w": nrm(ks[8], (DEPTH, CONV_B_WIDTH, W_SCONV), CONV_B_WIDTH ** -0.5),
        "qn_c": gain(ks[9], (DEPTH, HEAD_DIM)),
        "kn_c": gain(ks[10], (DEPTH, HEAD_DIM)),
        "qn_d": gain(ks[11], (DEPTH, HEAD_DIM)),
        "kn_d": gain(ks[12], (DEPTH, HEAD_DIM)),
        "w_out": nrm(ks[13], (DEPTH, D_MIX, D_MODEL), D_MIX ** -0.5),
        "norm2_g": gain(ks[14], (DEPTH, D_MODEL)),
        "w_group": nrm(ks[15], (DEPTH, D_MODEL, N_GROUPS), D_MODEL ** -0.5),
        "b_group": nrm(ks[16], (DEPTH, N_GROUPS), 0.01),
        "w_router": nrm(ks[17], (DEPTH, D_MODEL, N_EXPERTS), D_MODEL ** -0.5),
        "b_router": nrm(ks[18], (DEPTH, N_EXPERTS), 0.01),
        "w_gate": nrm(ks[19], (DEPTH, N_EXPERTS, D_MODEL, D_EXPERT), D_MODEL ** -0.5),
        "w_up": nrm(ks[20], (DEPTH, N_EXPERTS, D_MODEL, D_EXPERT), D_MODEL ** -0.5),
        "w_down": nrm(ks[21], (DEPTH, N_EXPERTS, D_EXPERT, D_MODEL), D_EXPERT ** -0.5),
    }


def reference(x_prompt, x_sample, norm1_g, w_in, conv_a_w, conv_a_b, ln_a_g, ln_a_b, conv_b_w,
              qn_c, kn_c, qn_d, kn_d, w_out, norm2_g, w_group, b_group,
              w_router, b_router, w_gate, w_up, w_down):
    y_prompt = trunk(x_prompt, norm1_g, w_in, conv_a_w, conv_a_b, ln_a_g, ln_a_b, conv_b_w,
                     qn_c, kn_c, qn_d, kn_d, w_out, norm2_g, w_group, b_group,
                     w_router, b_router, w_gate, w_up, w_down)
    y_sample = trunk(x_sample, norm1_g, w_in, conv_a_w, conv_a_b, ln_a_g, ln_a_b, conv_b_w,
                     qn_c, kn_c, qn_d, kn_d, w_out, norm2_g, w_group, b_group,
                     w_router, b_router, w_gate, w_up, w_down)
    return (y_prompt, y_sample)
```

```python
import functools
import math

import jax
import jax.numpy as jnp
from jax import lax
from jax.experimental import pallas as pl
from jax.experimental.pallas import tpu as pltpu

D_MODEL = 1024
HEAD_DIM = 64
W_MIX = 256
N_HEADS = 4
N_KV_HEADS = 2
CONV_A_WIDTH = 31
CONV_B_WIDTH = 3
DIL_PATTERNS = ((128, 1), (512, 4), (2048, 16))
Q_BLOCK = 128
GRID_W = 64
ROPE_THETA = 10000.0
N_GROUPS = 4
EXPERTS_PER_GROUP = 4
N_EXPERTS = 16
D_EXPERT = 512
EPS = 1e-6
NEG_INF = -1e30

SZ_A, SZ_B, SZ_C, SZ_D = 512, 768, 768, 512
D_IN = SZ_A + SZ_B + SZ_C + SZ_D
LANES = 128
HALO = 16
VMEM_LIMIT = 56 * 1024 * 1024

F32 = jnp.float32
BF16 = jnp.bfloat16


def _cparams(sem):
    return pltpu.CompilerParams(dimension_semantics=sem, vmem_limit_bytes=VMEM_LIMIT)


def _inproj_kernel(x_ref, g_ref, w_ref, pa_ref, pb_ref, pc_ref, pd_ref):
    x = x_ref[...]
    ms = jnp.mean(x * x, axis=-1, keepdims=True)
    h = (x * lax.rsqrt(ms + EPS) * g_ref[...]).astype(BF16)
    off = 0
    for ref, sz in ((pa_ref, SZ_A), (pb_ref, SZ_B), (pc_ref, SZ_C), (pd_ref, SZ_D)):
        ref[...] = jnp.dot(h, w_ref[:, off:off + sz], preferred_element_type=F32)
        off += sz


def _inproj(x2, g, w_bf16, tm):
    n = x2.shape[0]
    outs = [jax.ShapeDtypeStruct((n, sz), F32) for sz in (SZ_A, SZ_B, SZ_C, SZ_D)]
    return pl.pallas_call(
        _inproj_kernel,
        grid=(n // tm,),
        in_specs=[pl.BlockSpec((tm, D_MODEL), lambda i: (i, 0)),
                  pl.BlockSpec((1, D_MODEL), lambda i: (0, 0)),
                  pl.BlockSpec((D_MODEL, D_IN), lambda i: (0, 0))],
        out_specs=[pl.BlockSpec((tm, sz), lambda i: (i, 0)) for sz in (SZ_A, SZ_B, SZ_C, SZ_D)],
        out_shape=outs,
        compiler_params=_cparams(("parallel",)),
        name="inproj",
    )(x2, g.reshape(1, D_MODEL), w_bf16)


def _conv_kernel(pa_ref, pa_prev_ref, pa_next_ref, pb_ref, pb_prev_ref, pb_next_ref,
                 wa_ref, ba_ref, lg_ref, lb_ref, wb_ref, ua_ref, ub_ref, bufa, bufb, *, t):
    si = pl.program_id(1)
    ns = pl.num_programs(1)
    has_prev = (si > 0).astype(F32)
    has_next = (si < ns - 1).astype(F32)

    def glu(ref):
        p = ref[...]
        return p[:, :W_MIX] * jax.nn.sigmoid(p[:, W_MIX:])

    def gated(ref):
        p = ref[...]
        return p[:, W_MIX:2 * W_MIX] * p[:, 2 * W_MIX:]

    bufa[0:HALO, :] = glu(pa_prev_ref) * has_prev
    bufa[HALO:HALO + t, :] = glu(pa_ref)
    bufa[HALO + t:2 * HALO + t, :] = glu(pa_next_ref) * has_next
    bufb[0:HALO, :] = gated(pb_prev_ref) * has_prev
    bufb[HALO:HALO + t, :] = gated(pb_ref)
    bufb[HALO + t:2 * HALO + t, :] = gated(pb_next_ref) * has_next

    acc = jnp.zeros((t, W_MIX), F32)
    for k in range(CONV_A_WIDTH):
        start = HALO - CONV_A_WIDTH // 2 + k
        acc = acc + bufa[start:start + t, :] * wa_ref[k:k + 1, :]
    acc = acc + ba_ref[...]
    mu = jnp.mean(acc, axis=-1, keepdims=True)
    xc = acc - mu
    var = jnp.mean(xc * xc, axis=-1, keepdims=True)
    y = xc * lax.rsqrt(var + EPS) * lg_ref[...] + lb_ref[...]
    ua_ref[...] = (y * jax.nn.sigmoid(y)).astype(BF16)

    accb = jnp.zeros((t, W_MIX), F32)
    for k in range(CONV_B_WIDTH):
        start = HALO - CONV_B_WIDTH // 2 + k
        accb = accb + bufb[start:start + t, :] * wb_ref[k:k + 1, :]
    ub_ref[...] = (pb_ref[:, :W_MIX] * accb).astype(BF16)


def _conv_mixers(pa, pb, wa, ba, lg, lb, wb, t):
    nb, s, _ = pa.shape
    hb = t // HALO
    last = s // HALO - 1

    def main(b, i):
        return (b, i, 0)

    def prev(b, i):
        return (b, jnp.maximum(i * hb - 1, 0), 0)

    def nxt(b, i):
        return (b, jnp.minimum((i + 1) * hb, last), 0)

    const = lambda b, i: (0, 0)
    return pl.pallas_call(
        functools.partial(_conv_kernel, t=t),
        grid=(nb, s // t),
        in_specs=[pl.BlockSpec((None, t, SZ_A), main),
                  pl.BlockSpec((None, HALO, SZ_A), prev),
                  pl.BlockSpec((None, HALO, SZ_A), nxt),
                  pl.BlockSpec((None, t, SZ_B), main),
                  pl.BlockSpec((None, HALO, SZ_B), prev),
                  pl.BlockSpec((None, HALO, SZ_B), nxt),
                  pl.BlockSpec((CONV_A_WIDTH, W_MIX), const),
                  pl.BlockSpec((1, W_MIX), const),
                  pl.BlockSpec((1, W_MIX), const),
                  pl.BlockSpec((1, W_MIX), const),
                  pl.BlockSpec((CONV_B_WIDTH, W_MIX), const)],
        out_specs=[pl.BlockSpec((None, t, W_MIX), main),
                   pl.BlockSpec((None, t, W_MIX), main)],
        out_shape=[jax.ShapeDtypeStruct((nb, s, W_MIX), BF16)] * 2,
        scratch_shapes=[pltpu.VMEM((t + 2 * HALO, W_MIX), F32),
                        pltpu.VMEM((t + 2 * HALO, W_MIX), F32)],
        compiler_params=_cparams(("parallel", "parallel")),
        name="conv_mixers",
    )(pa, pa, pa, pb, pb, pb, wa, ba.reshape(1, W_MIX), lg.reshape(1, W_MIX),
      lb.reshape(1, W_MIX), wb)


def _head_sumsq(x, ones_bd):
    x2 = x * x
    hi = x2.astype(BF16)
    lo = (x2 - hi.astype(F32)).astype(BF16)
    return (jnp.dot(hi, ones_bd, preferred_element_type=F32)
            + jnp.dot(lo, ones_bd, preferred_element_type=F32))


def _rope(x, cos, sin_signed, half):
    w = x.shape[-1]
    lane = lax.broadcasted_iota(jnp.int32, x.shape, 1)
    fwd = pltpu.roll(x, w - half, 1)
    bwd = pltpu.roll(x, half, 1)
    rot = jnp.where(lane % (2 * half) < half, fwd, bwd)
    return x * cos + rot * sin_signed


def _prep_kernel(pc_ref, pd_ref, gains_ref, cosc_ref, sinc_ref, cosd_ref, sind_ref, ones_ref,
                 qc_ref, kc_ref, vc_ref, qd_ref, kd_ref, vd_ref):
    scale = HEAD_DIM ** -0.5
    ones_bd = ones_ref[...]
    pc = pc_ref[...]
    pd = pd_ref[...]
    gains = gains_ref[...]
    cosc, sinc = cosc_ref[...], sinc_ref[...]
    cosd, sind = cosd_ref[...], sind_ref[...]

    def normed(x, g, ones):
        ss = _head_sumsq(x, ones)
        return x * lax.rsqrt(ss * (1.0 / HEAD_DIM) + EPS) * g

    qc = _rope(normed(pc[:, 0:256], gains[0:1, :], ones_bd), cosc, sinc, HEAD_DIM // 2) * scale
    kc = _rope(normed(pc[:, 256:512], gains[1:2, :], ones_bd), cosc, sinc, HEAD_DIM // 2)
    qc_ref[...] = qc.astype(BF16)
    kc_ref[...] = kc.astype(BF16)
    vc_ref[...] = pc[:, 512:768].astype(BF16)

    qd = _rope(normed(pd[:, 0:256], gains[2:3, :], ones_bd), cosd, sind, HEAD_DIM // 4) * scale
    kd = _rope(normed(pd[:, 256:384], gains[3:4, 0:128], ones_bd[0:128, 0:128]),
               cosd[:, 0:128], sind[:, 0:128], HEAD_DIM // 4)
    vd = pd[:, 384:512]
    for h in range(N_HEADS):
        qd_ref[h] = qd[:, h * HEAD_DIM:(h + 1) * HEAD_DIM].astype(BF16)
    for h in range(N_KV_HEADS):
        kd_ref[h] = kd[:, h * HEAD_DIM:(h + 1) * HEAD_DIM].astype(BF16)
        vd_ref[h] = vd[:, h * HEAD_DIM:(h + 1) * HEAD_DIM].astype(BF16)


def _rope_tables(s):
    def angles(pos, dim):
        inv = ROPE_THETA ** (-jnp.arange(0, dim, 2, dtype=F32) / dim)
        return pos.astype(F32)[:, None] * inv[None, :]

    pos = jnp.arange(s)
    ang = angles(pos, HEAD_DIM)
    cos_c = jnp.tile(jnp.concatenate([jnp.cos(ang), jnp.cos(ang)], -1), (1, N_HEADS))
    sin_c = jnp.tile(jnp.concatenate([-jnp.sin(ang), jnp.sin(ang)], -1), (1, N_HEADS))
    row = pos // GRID_W
    col = pos % GRID_W
    ar = angles(row, HEAD_DIM // 2)
    ac = angles(col, HEAD_DIM // 2)
    cos_d = jnp.tile(jnp.concatenate([jnp.cos(ar), jnp.cos(ar), jnp.cos(ac), jnp.cos(ac)], -1),
                     (1, N_HEADS))
    sin_d = jnp.tile(jnp.concatenate([-jnp.sin(ar), jnp.sin(ar), -jnp.sin(ac), jnp.sin(ac)], -1),
                     (1, N_HEADS))
    return cos_c, sin_c, cos_d, sin_d


def _prep(pc, pd, gains, tables, ts):
    nb, s, _ = pc.shape
    cos_c, sin_c, cos_d, sin_d = tables
    head = lax.broadcasted_iota(jnp.int32, (W_MIX, W_MIX), 0) // HEAD_DIM
    ones_bd = (head == head.T).astype(BF16)
    tok = lambda b, i: (b, i, 0)
    tab = lambda b, i: (i, 0)
    const = lambda b, i: (0, 0)
    hm = lambda b, i: (b, 0, i, 0)
    return pl.pallas_call(
        _prep_kernel,
        grid=(nb, s // ts),
        in_specs=[pl.BlockSpec((None, ts, SZ_C), tok),
                  pl.BlockSpec((None, ts, SZ_D), tok),
                  pl.BlockSpec((4, W_MIX), const),
                  pl.BlockSpec((ts, W_MIX), tab),
                  pl.BlockSpec((ts, W_MIX), tab),
                  pl.BlockSpec((ts, W_MIX), tab),
                  pl.BlockSpec((ts, W_MIX), tab),
                  pl.BlockSpec((W_MIX, W_MIX), const)],
        out_specs=[pl.BlockSpec((None, ts, W_MIX), tok),
                   pl.BlockSpec((None, ts, W_MIX), tok),
                   pl.BlockSpec((None, ts, W_MIX), tok),
                   pl.BlockSpec((None, N_HEADS, ts, HEAD_DIM), hm),
                   pl.BlockSpec((None, N_KV_HEADS, ts, HEAD_DIM), hm),
                   pl.BlockSpec((None, N_KV_HEADS, ts, HEAD_DIM), hm)],
        out_shape=[jax.ShapeDtypeStruct((nb, s, W_MIX), BF16)] * 3
        + [jax.ShapeDtypeStruct((nb, N_HEADS, s, HEAD_DIM), BF16),
           jax.ShapeDtypeStruct((nb, N_KV_HEADS, s, HEAD_DIM), BF16),
           jax.ShapeDtypeStruct((nb, N_KV_HEADS, s, HEAD_DIM), BF16)],
        compiler_params=_cparams(("parallel", "parallel")),
        name="qk_prep",
    )(pc, pd, gains, cos_c, sin_c, cos_d, sin_d, ones_bd)


def _dilated_kernel(*refs, seq_sub, half, lq, has_prev, last):
    if has_prev:
        q_ref, k_ref, v_ref, po_ref, pl_ref = refs[:5]
        outs = refs[5:]
    else:
        q_ref, k_ref, v_ref = refs[:3]
        outs = refs[3:]
    span = Q_BLOCK + 2 * half
    qi = pl.program_id(2)
    nt = (((1,), (1,)), ((), ()))

    def body(i, carry):
        q0 = pl.multiple_of(i * Q_BLOCK, Q_BLOCK)
        qabs = qi * lq + q0
        ks = jnp.clip(qabs - half, 0, seq_sub - span)
        ks = pl.multiple_of(ks, half)
        q = q_ref[pl.ds(q0, Q_BLOCK), :]
        k = k_ref[pl.ds(ks, span), :]
        v = v_ref[pl.ds(ks, span), :]
        qpos = qabs + lax.broadcasted_iota(jnp.int32, (Q_BLOCK, span), 0)
        kpos = ks + lax.broadcasted_iota(jnp.int32, (Q_BLOCK, span), 1)
        mask = jnp.abs(kpos - qpos) <= half
        o_parts, l_parts = [], []
        for h in range(N_HEADS):
            hs = slice(h * HEAD_DIM, (h + 1) * HEAD_DIM)
            sc = lax.dot_general(q[:, hs], k[:, hs], nt, preferred_element_type=F32)
            sc = jnp.where(mask, sc, NEG_INF)
            m = jnp.max(sc, axis=-1, keepdims=True)
            p = jnp.exp(sc - m)
            l = jnp.sum(p, axis=-1, keepdims=True)
            num = jnp.dot(p.astype(BF16), v[:, hs], preferred_element_type=F32)
            o_parts.append(num / l)
            l_parts.append(jnp.broadcast_to(m + jnp.log(l), (Q_BLOCK, HEAD_DIM)))
        o = jnp.concatenate(o_parts, axis=-1)
        lse = jnp.concatenate(l_parts, axis=-1)
        if has_prev:
            po = po_ref[pl.ds(q0, Q_BLOCK), :]
            pls = pl_ref[pl.ds(q0, Q_BLOCK), :]
            mx = jnp.maximum(pls, lse)
            wp = jnp.exp(pls - mx)
            wc = jnp.exp(lse - mx)
            tot = wp + wc
            o = (wp * po + wc * o) / tot
            lse = mx + jnp.log(tot)
        if last:
            outs[0][pl.ds(q0, Q_BLOCK), :] = o.astype(outs[0].dtype)
        else:
            outs[0][pl.ds(q0, Q_BLOCK), :] = o
            outs[1][pl.ds(q0, Q_BLOCK), :] = lse
        return carry

    lax.fori_loop(0, lq // Q_BLOCK, body, 0)


def _dilated_pattern(q, k, v, prev, win, dil, last):
    nb, s, _ = q.shape
    half = win // (2 * dil)
    seq_sub = s // dil
    lq = min(seq_sub, 1024)
    view = lambda a: a.reshape(nb, seq_sub, dil * W_MIX)
    qspec = pl.BlockSpec((None, lq, W_MIX), lambda b, r, i: (b, i, r))
    kvspec = pl.BlockSpec((None, seq_sub, W_MIX), lambda b, r, i: (b, 0, r))
    ins = [view(q), view(k), view(v)]
    in_specs = [qspec, kvspec, kvspec]
    if prev is not None:
        ins += [view(prev[0]), view(prev[1])]
        in_specs += [qspec, qspec]
    if last:
        out_shape = [jax.ShapeDtypeStruct((nb, seq_sub, dil * W_MIX), BF16)]
        out_specs = [qspec]
    else:
        out_shape = [jax.ShapeDtypeStruct((nb, seq_sub, dil * W_MIX), F32)] * 2
        out_specs = [qspec, qspec]
    outs = pl.pallas_call(
        functools.partial(_dilated_kernel, seq_sub=seq_sub, half=half, lq=lq,
                          has_prev=prev is not None, last=last),
        grid=(nb, dil, seq_sub // lq),
        in_specs=in_specs,
        out_specs=out_specs,
        out_shape=out_shape,
        compiler_params=_cparams(("parallel", "parallel", "arbitrary")),
        name=f"dilated_w{win}_d{dil}",
    )(*ins)
    return [o.reshape(nb, s, W_MIX) for o in outs]


def _dilated_mixture(qc, kc, vc):
    prev = None
    for idx, (win, dil) in enumerate(DIL_PATTERNS):
        last = idx == len(DIL_PATTERNS) - 1
        prev = _dilated_pattern(qc, kc, vc, prev, win, dil, last)
    return prev[0]


def _gqa_kernel(q_ref, k_ref, v_ref, o_ref, *, tq, tk, s):
    nt = (((1,), (1,)), ((), ()))
    group = N_HEADS // N_KV_HEADS
    qs = [q_ref[j] for j in range(group)]

    def body(c, carry):
        k0 = pl.multiple_of(c * tk, tk)
        k = k_ref[pl.ds(k0, tk), :]
        v = v_ref[pl.ds(k0, tk), :]
        new = []
        for j in range(group):
            m, l, acc = carry[j]
            sc = lax.dot_general(qs[j], k, nt, preferred_element_type=F32)
            m_new = jnp.maximum(m, jnp.max(sc, axis=-1, keepdims=True))
            alpha = jnp.exp(m - m_new)
            p = jnp.exp(sc - m_new)
            l = alpha * l + jnp.sum(p, axis=-1, keepdims=True)
            acc = alpha * acc + jnp.dot(p.astype(BF16), v, preferred_element_type=F32)
            new.append((m_new, l, acc))
        return tuple(new)

    init = tuple((jnp.full((tq, 1), -jnp.inf, F32), jnp.zeros((tq, 1), F32),
                  jnp.zeros((tq, HEAD_DIM), F32)) for _ in range(group))
    fin = lax.fori_loop(0, s // tk, body, init)
    o_ref[...] = jnp.concatenate([acc / l for (_, l, acc) in fin], axis=-1).astype(o_ref.dtype)


def _gqa(qd, kd, vd, tq, tk):
    nb, _, s, _ = qd.shape
    group = N_HEADS // N_KV_HEADS
    return pl.pallas_call(
        functools.partial(_gqa_kernel, tq=tq, tk=tk, s=s),
        grid=(nb, N_KV_HEADS, s // tq),
        in_specs=[pl.BlockSpec((None, group, tq, HEAD_DIM), lambda b, g, i: (b, g, i, 0)),
                  pl.BlockSpec((None, None, s, HEAD_DIM), lambda b, g, i: (b, g, 0, 0)),
                  pl.BlockSpec((None, None, s, HEAD_DIM), lambda b, g, i: (b, g, 0, 0))],
        out_specs=pl.BlockSpec((None, tq, group * HEAD_DIM), lambda b, g, i: (b, i, g)),
        out_shape=jax.ShapeDtypeStruct((nb, s, W_MIX), BF16),
        compiler_params=_cparams(("parallel", "parallel", "arbitrary")),
        name="gqa",
    )(qd, kd, vd)


def _outproj_kernel(ua_ref, ub_ref, uc_ref, ud_ref, x_ref, w_ref, g_ref, wr_ref, br_ref,
                    x1_ref, xn_ref, gates_ref):
    acc = x_ref[...]
    for j, ref in enumerate((ua_ref, ub_ref, uc_ref, ud_ref)):
        acc = acc + jnp.dot(ref[...], w_ref[j * W_MIX:(j + 1) * W_MIX, :],
                            preferred_element_type=F32)
    x1_ref[...] = acc
    ms = jnp.mean(acc * acc, axis=-1, keepdims=True)
    xn = (acc * lax.rsqrt(ms + EPS) * g_ref[...]).astype(BF16)
    xn_ref[...] = xn
    logits = jnp.dot(xn, wr_ref[...], preferred_element_type=F32) + br_ref[...]
    gates_ref[...] = _route(logits)


def _route(logits):
    lane = lax.broadcasted_iota(jnp.int32, logits.shape, 1)
    big = jnp.int32(LANES)
    ninf = -jnp.inf

    def argmax(vals):
        mx = jnp.max(vals, axis=-1, keepdims=True)
        idx = jnp.min(jnp.where(vals == mx, lane, big), axis=-1, keepdims=True)
        return mx, idx

    glog = jnp.where(lane < N_GROUPS, logits, ninf)
    gmax, gidx = argmax(glog)
    g_w = 1.0 / jnp.sum(jnp.exp(glog - gmax), axis=-1, keepdims=True)
    lo = N_GROUPS + gidx * EXPERTS_PER_GROUP
    el = jnp.where((lane >= lo) & (lane < lo + EXPERTS_PER_GROUP), logits, ninf)
    t1, i1 = argmax(el)
    el2 = jnp.where(lane == i1, ninf, el)
    t2, i2 = argmax(el2)
    e2 = jnp.exp(t2 - t1)
    w1 = 1.0 / (1.0 + e2)
    w2 = e2 * w1
    return jnp.where(lane == i1, w1 * g_w, jnp.where(lane == i2, w2 * g_w, 0.0))


def _outproj(ua, ub, uc, ud, x2, w_bf16, g2, wr, br, tm):
    n = x2.shape[0]
    row = lambda i: (i, 0)
    const = lambda i: (0, 0)
    return pl.pallas_call(
        _outproj_kernel,
        grid=(n // tm,),
        in_specs=[pl.BlockSpec((tm, W_MIX), row)] * 4
        + [pl.BlockSpec((tm, D_MODEL), row),
           pl.BlockSpec((D_MODEL, D_MODEL), const),
           pl.BlockSpec((1, D_MODEL), const),
           pl.BlockSpec((D_MODEL, LANES), const),
           pl.BlockSpec((1, LANES), const)],
        out_specs=[pl.BlockSpec((tm, D_MODEL), row),
                   pl.BlockSpec((tm, D_MODEL), row),
                   pl.BlockSpec((tm, LANES), row)],
        out_shape=[jax.ShapeDtypeStruct((n, D_MODEL), F32),
                   jax.ShapeDtypeStruct((n, D_MODEL), BF16),
                   jax.ShapeDtypeStruct((n, LANES), F32)],
        compiler_params=_cparams(("parallel",)),
        name="outproj_router",
    )(ua, ub, uc, ud, x2, w_bf16, g2.reshape(1, D_MODEL), wr, br)


def _moe_kernel(xn_ref, gates_ref, x1_ref, wg_ref, wu_ref, wd_ref, o_ref):
    e = pl.program_id(1)

    @pl.when(e == 0)
    def _():
        o_ref[...] = x1_ref[...]

    xn = xn_ref[...]
    gates = gates_ref[...]
    lane = lax.broadcasted_iota(jnp.int32, gates.shape, 1)
    g = jnp.sum(jnp.where(lane == N_GROUPS + e, gates, 0.0), axis=-1, keepdims=True)
    a = jnp.dot(xn, wg_ref[...], preferred_element_type=F32)
    u = jnp.dot(xn, wu_ref[...], preferred_element_type=F32)
    hdn = (a * jax.nn.sigmoid(a) * u).astype(BF16)
    o_ref[...] += g * jnp.dot(hdn, wd_ref[...], preferred_element_type=F32)


def _moe(xn, gates, x1, wg, wu, wd, tm):
    n = xn.shape[0]
    row = lambda i, e: (i, 0)
    return pl.pallas_call(
        _moe_kernel,
        grid=(n // tm, N_EXPERTS),
        in_specs=[pl.BlockSpec((tm, D_MODEL), row),
                  pl.BlockSpec((tm, LANES), row),
                  pl.BlockSpec((tm, D_MODEL), row),
                  pl.BlockSpec((None, D_MODEL, D_EXPERT), lambda i, e: (e, 0, 0)),
                  pl.BlockSpec((None, D_MODEL, D_EXPERT), lambda i, e: (e, 0, 0)),
                  pl.BlockSpec((None, D_EXPERT, D_MODEL), lambda i, e: (e, 0, 0))],
        out_specs=pl.BlockSpec((tm, D_MODEL), row),
        out_shape=jax.ShapeDtypeStruct((n, D_MODEL), F32),
        compiler_params=_cparams(("parallel", "arbitrary")),
        name="moe",
    )(xn, gates, x1, wg, wu, wd)


def _tiles(nb, s):
    return dict(tm=min(512, s), tconv=min(512, s), tprep=min(512, s),
                tq=min(256, s), tk=min(512, s), tmoe=min(1024, s))


def _layer(x, p, l, tables, tl):
    nb, s, _ = x.shape
    n = nb * s
    x2 = x.reshape(n, D_MODEL)
    pa, pb, pc, pd = _inproj(x2, p["norm1_g"][l], p["w_in"][l], tl["tm"])
    pa = pa.reshape(nb, s, SZ_A)
    pb = pb.reshape(nb, s, SZ_B)
    pc = pc.reshape(nb, s, SZ_C)
    pd = pd.reshape(nb, s, SZ_D)
    ua, ub = _conv_mixers(pa, pb, p["conv_a_w"][l], p["conv_a_b"][l], p["ln_a_g"][l],
                          p["ln_a_b"][l], p["conv_b_w"][l], tl["tconv"])
    gains = jnp.stack([jnp.tile(p[k][l], N_HEADS) for k in ("qn_c", "kn_c", "qn_d", "kn_d")])
    qc, kc, vc, qd, kd, vd = _prep(pc, pd, gains, tables, tl["tprep"])
    uc = _dilated_mixture(qc, kc, vc)
    ud = _gqa(qd, kd, vd, tl["tq"], tl["tk"])
    x1, xn, gates = _outproj(ua.reshape(n, W_MIX), ub.reshape(n, W_MIX), uc.reshape(n, W_MIX),
                             ud.reshape(n, W_MIX), x2, p["w_out"][l], p["norm2_g"][l],
                             p["w_route"][l], p["b_route"][l], tl["tm"])
    y = _moe(xn, gates, x1, p["w_gate"][l], p["w_up"][l], p["w_down"][l], tl["tmoe"])
    return y.reshape(nb, s, D_MODEL)


def _trunk(x, p):
    nb, s, _ = x.shape
    tables = _rope_tables(s)
    tl = _tiles(nb, s)
    for l in range(p["w_in"].shape[0]):
        x = _layer(x, p, l, tables, tl)
    return x


def _pack_params(norm1_g, w_in, conv_a_w, conv_a_b, ln_a_g, ln_a_b, conv_b_w, qn_c, kn_c, qn_d, kn_d,
                 w_out, norm2_g, w_group, b_group, w_router, b_router, w_gate, w_up, w_down):
    depth = w_in.shape[0]
    pad = LANES - N_GROUPS - N_EXPERTS
    w_route = jnp.concatenate([w_group, w_router, jnp.zeros((depth, D_MODEL, pad), F32)], axis=-1)
    b_route = jnp.concatenate([b_group, b_router, jnp.zeros((depth, pad), F32)], axis=-1)
    return dict(norm1_g=norm1_g, w_in=w_in.astype(BF16), conv_a_w=conv_a_w, conv_a_b=conv_a_b,
                ln_a_g=ln_a_g, ln_a_b=ln_a_b, conv_b_w=conv_b_w, qn_c=qn_c, kn_c=kn_c, qn_d=qn_d,
                kn_d=kn_d, w_out=w_out.astype(BF16), norm2_g=norm2_g,
                w_route=w_route.astype(BF16), b_route=b_route.reshape(depth, 1, LANES),
                w_gate=w_gate.astype(BF16), w_up=w_up.astype(BF16), w_down=w_down.astype(BF16))


def kernel(x_prompt, x_sample, norm1_g, w_in, conv_a_w, conv_a_b, ln_a_g, ln_a_b, conv_b_w, qn_c, kn_c, qn_d, kn_d, w_out, norm2_g, w_group, b_group, w_router, b_router, w_gate, w_up, w_down):
    p = _pack_params(norm1_g, w_in, conv_a_w, conv_a_b, ln_a_g, ln_a_b, conv_b_w, qn_c, kn_c, qn_d,
                     kn_d, w_out, norm2_g, w_group, b_group, w_router, b_router, w_gate, w_up, w_down)
    assert x_prompt.shape[1:] == x_sample.shape[1:]
    nbp = x_prompt.shape[0]
    y = _trunk(jnp.concatenate([x_prompt, x_sample], axis=0), p)
    return (y[:nbp], y[nbp:])
```

```python
import functools
import math

import jax
import jax.numpy as jnp
from jax import lax
from jax.experimental import pallas as pl
from jax.experimental.pallas import tpu as pltpu

D_MODEL = 1024
HEAD_DIM = 64
W_MIX = 256
N_HEADS = 4
N_KV_HEADS = 2
CONV_A_WIDTH = 31
CONV_B_WIDTH = 3
DIL_PATTERNS = ((128, 1), (512, 4), (2048, 16))
Q_BLOCK = 128
GRID_W = 64
ROPE_THETA = 10000.0
N_GROUPS = 4
EXPERTS_PER_GROUP = 4
N_EXPERTS = 16
D_EXPERT = 512
EPS = 1e-6
NEG_INF = -1e30
LOG2E = math.log2(math.e)

SZ_A, SZ_B, SZ_C, SZ_D = 512, 768, 768, 512
D_IN = SZ_A + SZ_B + SZ_C + SZ_D
LANES = 128
HALO = 16
VMEM_LIMIT = 56 * 1024 * 1024

F32 = jnp.float32
BF16 = jnp.bfloat16


def _cparams(sem):
    return pltpu.CompilerParams(dimension_semantics=sem, vmem_limit_bytes=VMEM_LIMIT)


def _inproj_kernel(x_ref, g_ref, w_ref, pa_ref, pb_ref, pc_ref, pd_ref):
    x = x_ref[...]
    ms = jnp.mean(x * x, axis=-1, keepdims=True)
    h = (x * lax.rsqrt(ms + EPS) * g_ref[...]).astype(BF16)
    off = 0
    for ref, sz in ((pa_ref, SZ_A), (pb_ref, SZ_B), (pc_ref, SZ_C), (pd_ref, SZ_D)):
        ref[...] = jnp.dot(h, w_ref[:, off:off + sz], preferred_element_type=F32)
        off += sz


def _inproj(x2, g, w_bf16, tm):
    n = x2.shape[0]
    outs = [jax.ShapeDtypeStruct((n, sz), F32) for sz in (SZ_A, SZ_B, SZ_C, SZ_D)]
    return pl.pallas_call(
        _inproj_kernel,
        grid=(n // tm,),
        in_specs=[pl.BlockSpec((tm, D_MODEL), lambda i: (i, 0)),
                  pl.BlockSpec((1, D_MODEL), lambda i: (0, 0)),
                  pl.BlockSpec((D_MODEL, D_IN), lambda i: (0, 0))],
        out_specs=[pl.BlockSpec((tm, sz), lambda i: (i, 0)) for sz in (SZ_A, SZ_B, SZ_C, SZ_D)],
        out_shape=outs,
        compiler_params=_cparams(("parallel",)),
        name="inproj",
    )(x2, g.reshape(1, D_MODEL), w_bf16)


def _conv_kernel(pa_ref, pa_prev_ref, pa_next_ref, pb_ref, pb_prev_ref, pb_next_ref,
                 wa_ref, ba_ref, lg_ref, lb_ref, wb_ref, ua_ref, ub_ref, bufa, bufb, *, t):
    si = pl.program_id(1)
    ns = pl.num_programs(1)
    has_prev = (si > 0).astype(F32)
    has_next = (si < ns - 1).astype(F32)

    def glu(ref):
        p = ref[...]
        return p[:, :W_MIX] * jax.nn.sigmoid(p[:, W_MIX:])

    def gated(ref):
        p = ref[...]
        return p[:, W_MIX:2 * W_MIX] * p[:, 2 * W_MIX:]

    bufa[0:HALO, :] = glu(pa_prev_ref) * has_prev
    bufa[HALO:HALO + t, :] = glu(pa_ref)
    bufa[HALO + t:2 * HALO + t, :] = glu(pa_next_ref) * has_next
    bufb[0:HALO, :] = gated(pb_prev_ref) * has_prev
    bufb[HALO:HALO + t, :] = gated(pb_ref)
    bufb[HALO + t:2 * HALO + t, :] = gated(pb_next_ref) * has_next

    acc = jnp.zeros((t, W_MIX), F32)
    for k in range(CONV_A_WIDTH):
        start = HALO - CONV_A_WIDTH // 2 + k
        acc = acc + bufa[start:start + t, :] * wa_ref[k:k + 1, :]
    acc = acc + ba_ref[...]
    mu = jnp.mean(acc, axis=-1, keepdims=True)
    xc = acc - mu
    var = jnp.mean(xc * xc, axis=-1, keepdims=True)
    y = xc * lax.rsqrt(var + EPS) * lg_ref[...] + lb_ref[...]
    ua_ref[...] = (y * jax.nn.sigmoid(y)).astype(BF16)

    accb = jnp.zeros((t, W_MIX), F32)
    for k in range(CONV_B_WIDTH):
        start = HALO - CONV_B_WIDTH // 2 + k
        accb = accb + bufb[start:start + t, :] * wb_ref[k:k + 1, :]
    ub_ref[...] = (pb_ref[:, :W_MIX] * accb).astype(BF16)


def _conv_mixers(pa, pb, wa, ba, lg, lb, wb, t):
    nb, s, _ = pa.shape
    hb = t // HALO
    last = s // HALO - 1

    def main(b, i):
        return (b, i, 0)

    def prev(b, i):
        return (b, jnp.maximum(i * hb - 1, 0), 0)

    def nxt(b, i):
        return (b, jnp.minimum((i + 1) * hb, last), 0)

    const = lambda b, i: (0, 0)
    return pl.pallas_call(
        functools.partial(_conv_kernel, t=t),
        grid=(nb, s // t),
        in_specs=[pl.BlockSpec((None, t, SZ_A), main),
                  pl.BlockSpec((None, HALO, SZ_A), prev),
                  pl.BlockSpec((None, HALO, SZ_A), nxt),
                  pl.BlockSpec((None, t, SZ_B), main),
                  pl.BlockSpec((None, HALO, SZ_B), prev),
                  pl.BlockSpec((None, HALO, SZ_B), nxt),
                  pl.BlockSpec((CONV_A_WIDTH, W_MIX), const),
                  pl.BlockSpec((1, W_MIX), const),
                  pl.BlockSpec((1, W_MIX), const),
                  pl.BlockSpec((1, W_MIX), const),
                  pl.BlockSpec((CONV_B_WIDTH, W_MIX), const)],
        out_specs=[pl.BlockSpec((None, t, W_MIX), main),
                   pl.BlockSpec((None, t, W_MIX), main)],
        out_shape=[jax.ShapeDtypeStruct((nb, s, W_MIX), BF16)] * 2,
        scratch_shapes=[pltpu.VMEM((t + 2 * HALO, W_MIX), F32),
                        pltpu.VMEM((t + 2 * HALO, W_MIX), F32)],
        compiler_params=_cparams(("parallel", "parallel")),
        name="conv_mixers",
    )(pa, pa, pa, pb, pb, pb, wa, ba.reshape(1, W_MIX), lg.reshape(1, W_MIX),
      lb.reshape(1, W_MIX), wb)


def _head_sumsq(x, ones_bd):
    x2 = x * x
    hi = x2.astype(BF16)
    lo = (x2 - hi.astype(F32)).astype(BF16)
    return (jnp.dot(hi, ones_bd, preferred_element_type=F32)
            + jnp.dot(lo, ones_bd, preferred_element_type=F32))


def _rope(x, cos, sin_signed, half):
    w = x.shape[-1]
    lane = lax.broadcasted_iota(jnp.int32, x.shape, 1)
    fwd = pltpu.roll(x, w - half, 1)
    bwd = pltpu.roll(x, half, 1)
    rot = jnp.where(lane % (2 * half) < half, fwd, bwd)
    return x * cos + rot * sin_signed


def _prep_kernel(pc_ref, pd_ref, gains_ref, cosc_ref, sinc_ref, cosd_ref, sind_ref, ones_ref,
                 c1_ref, c4_ref, c16_ref, qd_ref, kd_ref, vd_ref, qkv_scr):
    c_refs = (c1_ref, c4_ref, c16_ref)
    scale = HEAD_DIM ** -0.5
    ones_bd = ones_ref[...]
    pc = pc_ref[...]
    pd = pd_ref[...]
    gains = gains_ref[...]
    cosc, sinc = cosc_ref[...], sinc_ref[...]
    cosd, sind = cosd_ref[...], sind_ref[...]

    def normed(x, g, ones):
        ss = _head_sumsq(x, ones)
        return x * lax.rsqrt(ss * (1.0 / HEAD_DIM) + EPS) * g

    qc = _rope(normed(pc[:, 0:256], gains[0:1, :], ones_bd), cosc, sinc, HEAD_DIM // 2) * scale
    kc = _rope(normed(pc[:, 256:512], gains[1:2, :], ones_bd), cosc, sinc, HEAD_DIM // 2)
    qkv = (qc, kc, pc[:, 512:768])
    ncb = SZ_C // LANES
    for c in range(ncb):
        part = qkv[c * LANES // W_MIX]
        lo = c * LANES % W_MIX
        qkv_scr[c] = part[:, lo:lo + LANES]
    ts = pc.shape[0]
    for ref, (_, dil) in zip(c_refs, DIL_PATTERNS):
        for r in range(dil):
            for c in range(ncb):
                ref[r, :, c * LANES:(c + 1) * LANES] = (
                    qkv_scr[c, pl.ds(r, ts // dil, stride=dil), :].astype(BF16))

    qd = _rope(normed(pd[:, 0:256], gains[2:3, :], ones_bd), cosd, sind, HEAD_DIM // 4) * (scale * LOG2E)
    kd = _rope(normed(pd[:, 256:384], gains[3:4, 0:128], ones_bd[0:128, 0:128]),
               cosd[:, 0:128], sind[:, 0:128], HEAD_DIM // 4)
    vd = pd[:, 384:512]
    group = N_HEADS // N_KV_HEADS
    for h in range(N_HEADS):
        qd_ref[h // group, h % group] = qd[:, h * HEAD_DIM:(h + 1) * HEAD_DIM].astype(BF16)
    lane = lax.broadcasted_iota(jnp.int32, vd.shape, 1)
    kd_t = kd.T.astype(BF16)
    for h in range(N_KV_HEADS):
        kd_ref[h] = kd_t[h * HEAD_DIM:(h + 1) * HEAD_DIM, :]
        vh = vd if h == 0 else pltpu.roll(vd, LANES - h * HEAD_DIM, 1)
        vd_ref[h] = jnp.where(lane < HEAD_DIM, vh, jnp.where(lane == HEAD_DIM, 1.0, 0.0)).astype(BF16)


def _rope_tables(s):
    def angles(pos, dim):
        inv = ROPE_THETA ** (-jnp.arange(0, dim, 2, dtype=F32) / dim)
        return pos.astype(F32)[:, None] * inv[None, :]

    pos = jnp.arange(s)
    ang = angles(pos, HEAD_DIM)
    cos_c = jnp.tile(jnp.concatenate([jnp.cos(ang), jnp.cos(ang)], -1), (1, N_HEADS))
    sin_c = jnp.tile(jnp.concatenate([-jnp.sin(ang), jnp.sin(ang)], -1), (1, N_HEADS))
    row = pos // GRID_W
    col = pos % GRID_W
    ar = angles(row, HEAD_DIM // 2)
    ac = angles(col, HEAD_DIM // 2)
    cos_d = jnp.tile(jnp.concatenate([jnp.cos(ar), jnp.cos(ar), jnp.cos(ac), jnp.cos(ac)], -1),
                     (1, N_HEADS))
    sin_d = jnp.tile(jnp.concatenate([-jnp.sin(ar), jnp.sin(ar), -jnp.sin(ac), jnp.sin(ac)], -1),
                     (1, N_HEADS))
    return cos_c, sin_c, cos_d, sin_d


def _prep(pc, pd, gains, tables, ts):
    nb, s, _ = pc.shape
    cos_c, sin_c, cos_d, sin_d = tables
    head = lax.broadcasted_iota(jnp.int32, (W_MIX, W_MIX), 0) // HEAD_DIM
    ones_bd = (head == head.T).astype(BF16)
    tok = lambda b, i: (b, i, 0)
    tab = lambda b, i: (i, 0)
    const = lambda b, i: (0, 0)
    hm = lambda b, i: (b, 0, i, 0)
    group = N_HEADS // N_KV_HEADS
    return pl.pallas_call(
        _prep_kernel,
        grid=(nb, s // ts),
        in_specs=[pl.BlockSpec((None, ts, SZ_C), tok),
                  pl.BlockSpec((None, ts, SZ_D), tok),
                  pl.BlockSpec((4, W_MIX), const),
                  pl.BlockSpec((ts, W_MIX), tab),
                  pl.BlockSpec((ts, W_MIX), tab),
                  pl.BlockSpec((ts, W_MIX), tab),
                  pl.BlockSpec((ts, W_MIX), tab),
                  pl.BlockSpec((W_MIX, W_MIX), const)],
        out_specs=[pl.BlockSpec((None, dil, ts // dil, SZ_C), hm) for _, dil in DIL_PATTERNS]
        + [pl.BlockSpec((None, N_KV_HEADS, None, group, ts, HEAD_DIM),
                                lambda b, i: (b, 0, i, 0, 0, 0)),
                   pl.BlockSpec((None, N_KV_HEADS, HEAD_DIM, ts), lambda b, i: (b, 0, 0, i)),
                   pl.BlockSpec((None, N_KV_HEADS, ts, LANES), hm)],
        out_shape=[jax.ShapeDtypeStruct((nb, dil, s // dil, SZ_C), BF16) for _, dil in DIL_PATTERNS]
        + [jax.ShapeDtypeStruct((nb, N_KV_HEADS, s // ts, group, ts, HEAD_DIM), BF16),
           jax.ShapeDtypeStruct((nb, N_KV_HEADS, HEAD_DIM, s), BF16),
           jax.ShapeDtypeStruct((nb, N_KV_HEADS, s, LANES), BF16)],
        scratch_shapes=[pltpu.VMEM((SZ_C // LANES, ts, LANES), F32)],
        compiler_params=_cparams(("parallel", "parallel")),
        name="qk_prep",
    )(pc, pd, gains, cos_c, sin_c, cos_d, sin_d, ones_bd)


def _dilated_kernel(q_ref, k_ref, v_ref, o_ref, lse_ref, *, seq_sub, half, lq):
    span = Q_BLOCK + 2 * half
    qi = pl.program_id(2)
    nt = (((1,), (1,)), ((), ()))
    for i in range(lq // Q_BLOCK):
        q0 = i * Q_BLOCK
        qabs = qi * lq + q0
        ks = pl.multiple_of(jnp.clip(qabs - half, 0, seq_sub - span), half)
        q = q_ref[q0:q0 + Q_BLOCK, :]
        k = k_ref[pl.ds(ks, span), :]
        v = v_ref[pl.ds(ks, span), :]
        qpos = qabs + lax.broadcasted_iota(jnp.int32, (Q_BLOCK, span), 0)
        kpos = ks + lax.broadcasted_iota(jnp.int32, (Q_BLOCK, span), 1)
        mask = jnp.abs(kpos - qpos) <= half
        o_parts, l_parts = [], []
        for h in range(N_HEADS):
            hs = slice(h * HEAD_DIM, (h + 1) * HEAD_DIM)
            sc = lax.dot_general(q[:, hs], k[:, hs], nt, preferred_element_type=F32)
            sc = jnp.where(mask, sc, NEG_INF)
            m = jnp.max(sc, axis=-1, keepdims=True)
            p = jnp.exp(sc - m)
            l = jnp.sum(p, axis=-1, keepdims=True)
            num = jnp.dot(p.astype(BF16), v[:, hs], preferred_element_type=F32)
            o_parts.append(num / l)
            l_parts.append(jnp.broadcast_to(m + jnp.log(l), (Q_BLOCK, HEAD_DIM)))
        o_ref[q0:q0 + Q_BLOCK, :] = jnp.concatenate(o_parts, axis=-1)
        lse_ref[q0:q0 + Q_BLOCK, :] = jnp.concatenate(l_parts, axis=-1)


def _dilated_pattern(qkv, win, dil):
    nb, _, seq_sub, _ = qkv.shape
    half = win // (2 * dil)
    lq = min(seq_sub, 512)
    qspec = pl.BlockSpec((None, None, lq, W_MIX), lambda b, r, i: (b, r, i, 0))
    kspec = pl.BlockSpec((None, None, seq_sub, W_MIX), lambda b, r, i: (b, r, 0, 1))
    vspec = pl.BlockSpec((None, None, seq_sub, W_MIX), lambda b, r, i: (b, r, 0, 2))
    return pl.pallas_call(
        functools.partial(_dilated_kernel, seq_sub=seq_sub, half=half, lq=lq),
        grid=(nb, dil, seq_sub // lq),
        in_specs=[qspec, kspec, vspec],
        out_specs=[qspec, qspec],
        out_shape=[jax.ShapeDtypeStruct((nb, dil, seq_sub, W_MIX), F32)] * 2,
        compiler_params=_cparams(("parallel", "parallel", "arbitrary")),
        name=f"dilated_w{win}_d{dil}",
    )(qkv, qkv, qkv)


def _mixture_kernel(*refs, ts):
    n = len(DIL_PATTERNS)
    o_refs, l_refs = refs[0:2 * n:2], refs[1:2 * n:2]
    uc_ref = refs[2 * n]
    scr = refs[2 * n + 1:]

    def natural(ref, dil, buf):
        if dil == 1:
            return ref[0]
        ncb = W_MIX // LANES
        for r in range(dil):
            for c in range(ncb):
                buf[c, pl.ds(r, ts // dil, stride=dil), :] = ref[r, :, c * LANES:(c + 1) * LANES]
        return jnp.concatenate([buf[c] for c in range(ncb)], axis=-1)

    os_, ls_ = [], []
    for j, (_, dil) in enumerate(DIL_PATTERNS):
        os_.append(natural(o_refs[j], dil, scr[2 * j]))
        ls_.append(natural(l_refs[j], dil, scr[2 * j + 1]))
    mx = functools.reduce(jnp.maximum, ls_)
    ws = [jnp.exp(l - mx) for l in ls_]
    num = sum(w * o for w, o in zip(ws, os_))
    uc_ref[...] = (num / sum(ws)).astype(uc_ref.dtype)


def _dilated_mixture(c_layouts, ts):
    nb, _, s, _ = c_layouts[0].shape
    ins, in_specs = [], []
    for qkv, (win, dil) in zip(c_layouts, DIL_PATTERNS):
        o, lse = _dilated_pattern(qkv, win, dil)
        spec = pl.BlockSpec((None, dil, ts // dil, W_MIX), lambda b, i: (b, 0, i, 0))
        ins += [o, lse]
        in_specs += [spec, spec]
    return pl.pallas_call(
        functools.partial(_mixture_kernel, ts=ts),
        grid=(nb, s // ts),
        in_specs=in_specs,
        out_specs=pl.BlockSpec((None, ts, W_MIX), lambda b, i: (b, i, 0)),
        out_shape=jax.ShapeDtypeStruct((nb, s, W_MIX), BF16),
        scratch_shapes=[pltpu.VMEM((W_MIX // LANES, ts, LANES), F32)] * (2 * len(DIL_PATTERNS)),
        compiler_params=_cparams(("parallel", "parallel")),
        name="dilated_mixture",
    )(*ins)


def _gqa_kernel(q_ref, k_ref, v_ref, o_ref, *, tq, tk, s):
    group = N_HEADS // N_KV_HEADS
    rows = group * tq
    qs = [q_ref[g] for g in range(N_KV_HEADS)]

    def body(c, carry):
        k0 = pl.multiple_of(c * tk, tk)
        new = []
        for g in range(N_KV_HEADS):
            kt = k_ref[g, :, pl.ds(k0, tk)]
            v = v_ref[g, pl.ds(k0, tk), :]
            m, acc = carry[g]
            sc = jnp.dot(qs[g], kt, preferred_element_type=F32)
            m_new = jnp.maximum(m, jnp.max(sc, axis=-1, keepdims=True))
            alpha = jnp.exp2(m - m_new)
            p = jnp.exp2(sc - m_new).astype(BF16)
            acc = alpha * acc + jnp.dot(p, v, preferred_element_type=F32)
            new.append((m_new, acc))
        return tuple(new)

    init = tuple((jnp.full((rows, 1), -jnp.inf, F32), jnp.zeros((rows, LANES), F32))
                 for _ in range(N_KV_HEADS))
    fin = lax.fori_loop(0, s // tk, body, init)
    outs = []
    for (_, acc) in fin:
        o = acc[:, :HEAD_DIM] / acc[:, HEAD_DIM:HEAD_DIM + 1]
        outs += [o[j * tq:(j + 1) * tq, :] for j in range(group)]
    o_ref[...] = jnp.concatenate(outs, axis=-1).astype(o_ref.dtype)


def _gqa(qd, kd, vd, tq, tk):
    nb, _, nq, group, _, _ = qd.shape
    s = nq * tq
    qd = qd.reshape(nb, N_KV_HEADS, nq, group * tq, HEAD_DIM)
    return pl.pallas_call(
        functools.partial(_gqa_kernel, tq=tq, tk=tk, s=s),
        grid=(nb, nq),
        in_specs=[pl.BlockSpec((None, N_KV_HEADS, None, group * tq, HEAD_DIM),
                               lambda b, i: (b, 0, i, 0, 0)),
                  pl.BlockSpec((None, N_KV_HEADS, HEAD_DIM, s), lambda b, i: (b, 0, 0, 0)),
                  pl.BlockSpec((None, N_KV_HEADS, s, LANES), lambda b, i: (b, 0, 0, 0))],
        out_specs=pl.BlockSpec((None, tq, W_MIX), lambda b, i: (b, i, 0)),
        out_shape=jax.ShapeDtypeStruct((nb, s, W_MIX), BF16),
        compiler_params=_cparams(("parallel", "arbitrary")),
        name="gqa",
    )(qd, kd, vd)


def _outproj_kernel(ua_ref, ub_ref, uc_ref, ud_ref, x_ref, w_ref, g_ref, wr_ref, br_ref,
                    x1_ref, xn_ref, gates_ref):
    acc = x_ref[...]
    for j, ref in enumerate((ua_ref, ub_ref, uc_ref, ud_ref)):
        acc = acc + jnp.dot(ref[...], w_ref[j * W_MIX:(j + 1) * W_MIX, :],
                            preferred_element_type=F32)
    x1_ref[...] = acc
    ms = jnp.mean(acc * acc, axis=-1, keepdims=True)
    xn = (acc * lax.rsqrt(ms + EPS) * g_ref[...]).astype(BF16)
    xn_ref[...] = xn
    logits = jnp.dot(xn, wr_ref[...], preferred_element_type=F32) + br_ref[...]
    gates_ref[...] = _route(logits)


def _route(logits):
    lane = lax.broadcasted_iota(jnp.int32, logits.shape, 1)
    big = jnp.int32(LANES)
    ninf = -jnp.inf

    def argmax(vals):
        mx = jnp.max(vals, axis=-1, keepdims=True)
        idx = jnp.min(jnp.where(vals == mx, lane, big), axis=-1, keepdims=True)
        return mx, idx

    glog = jnp.where(lane < N_GROUPS, logits, ninf)
    gmax, gidx = argmax(glog)
    g_w = 1.0 / jnp.sum(jnp.exp(glog - gmax), axis=-1, keepdims=True)
    lo = N_GROUPS + gidx * EXPERTS_PER_GROUP
    el = jnp.where((lane >= lo) & (lane < lo + EXPERTS_PER_GROUP), logits, ninf)
    t1, i1 = argmax(el)
    el2 = jnp.where(lane == i1, ninf, el)
    t2, i2 = argmax(el2)
    e2 = jnp.exp(t2 - t1)
    w1 = 1.0 / (1.0 + e2)
    w2 = e2 * w1
    return jnp.where(lane == i1, w1 * g_w, jnp.where(lane == i2, w2 * g_w, 0.0))


def _outproj(ua, ub, uc, ud, x2, w_bf16, g2, wr, br, tm):
    n = x2.shape[0]
    row = lambda i: (i, 0)
    const = lambda i: (0, 0)
    return pl.pallas_call(
        _outproj_kernel,
        grid=(n // tm,),
        in_specs=[pl.BlockSpec((tm, W_MIX), row)] * 4
        + [pl.BlockSpec((tm, D_MODEL), row),
           pl.BlockSpec((D_MODEL, D_MODEL), const),
           pl.BlockSpec((1, D_MODEL), const),
           pl.BlockSpec((D_MODEL, LANES), const),
           pl.BlockSpec((1, LANES), const)],
        out_specs=[pl.BlockSpec((tm, D_MODEL), row),
                   pl.BlockSpec((tm, D_MODEL), row),
                   pl.BlockSpec((tm, LANES), row)],
        out_shape=[jax.ShapeDtypeStruct((n, D_MODEL), F32),
                   jax.ShapeDtypeStruct((n, D_MODEL), BF16),
                   jax.ShapeDtypeStruct((n, LANES), F32)],
        compiler_params=_cparams(("parallel",)),
        name="outproj_router",
    )(ua, ub, uc, ud, x2, w_bf16, g2.reshape(1, D_MODEL), wr, br)


def _moe_kernel(xn_ref, gates_ref, x1_ref, wg_ref, wu_ref, wd_ref, o_ref):
    e = pl.program_id(1)

    @pl.when(e == 0)
    def _():
        o_ref[...] = x1_ref[...]

    xn = xn_ref[...]
    gates = gates_ref[...]
    lane = lax.broadcasted_iota(jnp.int32, gates.shape, 1)
    g = jnp.sum(jnp.where(lane == N_GROUPS + e, gates, 0.0), axis=-1, keepdims=True)
    a = jnp.dot(xn, wg_ref[...], preferred_element_type=F32)
    u = jnp.dot(xn, wu_ref[...], preferred_element_type=F32)
    hdn = (a * jax.nn.sigmoid(a) * u).astype(BF16)
    o_ref[...] += g * jnp.dot(hdn, wd_ref[...], preferred_element_type=F32)


def _moe(xn, gates, x1, wg, wu, wd, tm):
    n = xn.shape[0]
    row = lambda i, e: (i, 0)
    return pl.pallas_call(
        _moe_kernel,
        grid=(n // tm, N_EXPERTS),
        in_specs=[pl.BlockSpec((tm, D_MODEL), row),
                  pl.BlockSpec((tm, LANES), row),
                  pl.BlockSpec((tm, D_MODEL), row),
                  pl.BlockSpec((None, D_MODEL, D_EXPERT), lambda i, e: (e, 0, 0)),
                  pl.BlockSpec((None, D_MODEL, D_EXPERT), lambda i, e: (e, 0, 0)),
                  pl.BlockSpec((None, D_EXPERT, D_MODEL), lambda i, e: (e, 0, 0))],
        out_specs=pl.BlockSpec((tm, D_MODEL), row),
        out_shape=jax.ShapeDtypeStruct((n, D_MODEL), F32),
        compiler_params=_cparams(("parallel", "arbitrary")),
        name="moe",
    )(xn, gates, x1, wg, wu, wd)


def _tiles(nb, s):
    return dict(tm=min(512, s), tconv=min(512, s),
                tq=min(512, s), tk=min(2048, s), tmoe=min(1024, s))


def _layer(x, p, l, tables, tl):
    nb, s, _ = x.shape
    n = nb * s
    x2 = x.reshape(n, D_MODEL)
    pa, pb, pc, pd = _inproj(x2, p["norm1_g"][l], p["w_in"][l], tl["tm"])
    pa = pa.reshape(nb, s, SZ_A)
    pb = pb.reshape(nb, s, SZ_B)
    pc = pc.reshape(nb, s, SZ_C)
    pd = pd.reshape(nb, s, SZ_D)
    ua, ub = _conv_mixers(pa, pb, p["conv_a_w"][l], p["conv_a_b"][l], p["ln_a_g"][l],
                          p["ln_a_b"][l], p["conv_b_w"][l], tl["tconv"])
    gains = jnp.stack([jnp.tile(p[k][l], N_HEADS) for k in ("qn_c", "kn_c", "qn_d", "kn_d")])
    c1, c4, c16, qd, kd, vd = _prep(pc, pd, gains, tables, tl["tq"])
    uc = _dilated_mixture((c1, c4, c16), tl["tm"])
    ud = _gqa(qd, kd, vd, tl["tq"], tl["tk"])
    x1, xn, gates = _outproj(ua.reshape(n, W_MIX), ub.reshape(n, W_MIX), uc.reshape(n, W_MIX),
                             ud.reshape(n, W_MIX), x2, p["w_out"][l], p["norm2_g"][l],
                             p["w_route"][l], p["b_route"][l], tl["tm"])
    y = _moe(xn, gates, x1, p["w_gate"][l], p["w_up"][l], p["w_down"][l], tl["tmoe"])
    return y.reshape(nb, s, D_MODEL)


def _trunk(x, p):
    nb, s, _ = x.shape
    tables = _rope_tables(s)
    tl = _tiles(nb, s)
    for l in range(p["w_in"].shape[0]):
        x = _layer(x, p, l, tables, tl)
    return x


def _pack_params(norm1_g, w_in, conv_a_w, conv_a_b, ln_a_g, ln_a_b, conv_b_w, qn_c, kn_c, qn_d, kn_d,
                 w_out, norm2_g, w_group, b_group, w_router, b_router, w_gate, w_up, w_down):
    depth = w_in.shape[0]
    pad = LANES - N_GROUPS - N_EXPERTS
    w_route = jnp.concatenate([w_group, w_router, jnp.zeros((depth, D_MODEL, pad), F32)], axis=-1)
    b_route = jnp.concatenate([b_group, b_router, jnp.zeros((depth, pad), F32)], axis=-1)
    return dict(norm1_g=norm1_g, w_in=w_in.astype(BF16), conv_a_w=conv_a_w, conv_a_b=conv_a_b,
                ln_a_g=ln_a_g, ln_a_b=ln_a_b, conv_b_w=conv_b_w, qn_c=qn_c, kn_c=kn_c, qn_d=qn_d,
                kn_d=kn_d, w_out=w_out.astype(BF16), norm2_g=norm2_g,
                w_route=w_route.astype(BF16), b_route=b_route.reshape(depth, 1, LANES),
                w_gate=w_gate.astype(BF16), w_up=w_up.astype(BF16), w_down=w_down.astype(BF16))


def kernel(x_prompt, x_sample, norm1_g, w_in, conv_a_w, conv_a_b, ln_a_g, ln_a_b, conv_b_w, qn_c, kn_c, qn_d, kn_d, w_out, norm2_g, w_group, b_group, w_router, b_router, w_gate, w_up, w_down):
    p = _pack_params(norm1_g, w_in, conv_a_w, conv_a_b, ln_a_g, ln_a_b, conv_b_w, qn_c, kn_c, qn_d,
                     kn_d, w_out, norm2_g, w_group, b_group, w_router, b_router, w_gate, w_up, w_down)
    assert x_prompt.shape[1:] == x_sample.shape[1:]
    nbp = x_prompt.shape[0]
    y = _trunk(jnp.concatenate([x_prompt, x_sample], axis=0), p)
    return (y[:nbp], y[nbp:])
```

```python
import functools
import math

import jax
import jax.numpy as jnp
from jax import lax
from jax.experimental import pallas as pl
from jax.experimental.pallas import tpu as pltpu

D_MODEL = 1024
HEAD_DIM = 64
W_MIX = 256
N_HEADS = 4
N_KV_HEADS = 2
CONV_A_WIDTH = 31
CONV_B_WIDTH = 3
DIL_PATTERNS = ((128, 1), (512, 4), (2048, 16))
Q_BLOCK = 128
GRID_W = 64
ROPE_THETA = 10000.0
N_GROUPS = 4
EXPERTS_PER_GROUP = 4
N_EXPERTS = 16
D_EXPERT = 512
EPS = 1e-6
NEG_INF = -1e30
LOG2E = math.log2(math.e)

SZ_A, SZ_B, SZ_C, SZ_D = 512, 768, 768, 512
D_IN = SZ_A + SZ_B + SZ_C + SZ_D
LANES = 128
D_ROW = D_MODEL + LANES
HALO = 16
VMEM_LIMIT = 56 * 1024 * 1024

F32 = jnp.float32
BF16 = jnp.bfloat16


def _cparams(sem):
    return pltpu.CompilerParams(dimension_semantics=sem, vmem_limit_bytes=VMEM_LIMIT)


def _inproj_kernel(x_ref, g_ref, w_ref, pa_ref, pb_ref, pc_ref, pd_ref):
    x = x_ref[...]
    ms = jnp.mean(x * x, axis=-1, keepdims=True)
    h = (x * lax.rsqrt(ms + EPS) * g_ref[...]).astype(BF16)
    off = 0
    for ref, sz in ((pa_ref, SZ_A), (pb_ref, SZ_B), (pc_ref, SZ_C), (pd_ref, SZ_D)):
        ref[...] = jnp.dot(h, w_ref[:, off:off + sz], preferred_element_type=F32)
        off += sz


def _inproj(x2, g, w_bf16, tm):
    n = x2.shape[0]
    outs = [jax.ShapeDtypeStruct((n, sz), F32) for sz in (SZ_A, SZ_B, SZ_C, SZ_D)]
    return pl.pallas_call(
        _inproj_kernel,
        grid=(n // tm,),
        in_specs=[pl.BlockSpec((tm, D_MODEL), lambda i: (i, 0)),
                  pl.BlockSpec((1, D_MODEL), lambda i: (0, 0)),
                  pl.BlockSpec((D_MODEL, D_IN), lambda i: (0, 0))],
        out_specs=[pl.BlockSpec((tm, sz), lambda i: (i, 0)) for sz in (SZ_A, SZ_B, SZ_C, SZ_D)],
        out_shape=outs,
        compiler_params=_cparams(("parallel",)),
        name="inproj",
    )(x2, g.reshape(1, D_MODEL), w_bf16)


def _conv_kernel(pa_ref, pa_prev_ref, pa_next_ref, pb_ref, pb_prev_ref, pb_next_ref,
                 wa_ref, ba_ref, lg_ref, lb_ref, wb_ref, ua_ref, ub_ref, bufa, bufb, *, t):
    si = pl.program_id(1)
    ns = pl.num_programs(1)
    has_prev = (si > 0).astype(F32)
    has_next = (si < ns - 1).astype(F32)

    def glu(ref):
        p = ref[...]
        return p[:, :W_MIX] * jax.nn.sigmoid(p[:, W_MIX:])

    def gated(ref):
        p = ref[...]
        return p[:, W_MIX:2 * W_MIX] * p[:, 2 * W_MIX:]

    bufa[0:HALO, :] = glu(pa_prev_ref) * has_prev
    bufa[HALO:HALO + t, :] = glu(pa_ref)
    bufa[HALO + t:2 * HALO + t, :] = glu(pa_next_ref) * has_next
    bufb[0:HALO, :] = gated(pb_prev_ref) * has_prev
    bufb[HALO:HALO + t, :] = gated(pb_ref)
    bufb[HALO + t:2 * HALO + t, :] = gated(pb_next_ref) * has_next

    acc = jnp.zeros((t, W_MIX), F32)
    for k in range(CONV_A_WIDTH):
        start = HALO - CONV_A_WIDTH // 2 + k
        acc = acc + bufa[start:start + t, :] * wa_ref[k:k + 1, :]
    acc = acc + ba_ref[...]
    mu = jnp.mean(acc, axis=-1, keepdims=True)
    xc = acc - mu
    var = jnp.mean(xc * xc, axis=-1, keepdims=True)
    y = xc * lax.rsqrt(var + EPS) * lg_ref[...] + lb_ref[...]
    ua_ref[...] = (y * jax.nn.sigmoid(y)).astype(BF16)

    accb = jnp.zeros((t, W_MIX), F32)
    for k in range(CONV_B_WIDTH):
        start = HALO - CONV_B_WIDTH // 2 + k
        accb = accb + bufb[start:start + t, :] * wb_ref[k:k + 1, :]
    ub_ref[...] = (pb_ref[:, :W_MIX] * accb).astype(BF16)


def _conv_mixers(pa, pb, wa, ba, lg, lb, wb, t):
    nb, s, _ = pa.shape
    hb = t // HALO
    last = s // HALO - 1

    def main(b, i):
        return (b, i, 0)

    def prev(b, i):
        return (b, jnp.maximum(i * hb - 1, 0), 0)

    def nxt(b, i):
        return (b, jnp.minimum((i + 1) * hb, last), 0)

    const = lambda b, i: (0, 0)
    return pl.pallas_call(
        functools.partial(_conv_kernel, t=t),
        grid=(nb, s // t),
        in_specs=[pl.BlockSpec((None, t, SZ_A), main),
                  pl.BlockSpec((None, HALO, SZ_A), prev),
                  pl.BlockSpec((None, HALO, SZ_A), nxt),
                  pl.BlockSpec((None, t, SZ_B), main),
                  pl.BlockSpec((None, HALO, SZ_B), prev),
                  pl.BlockSpec((None, HALO, SZ_B), nxt),
                  pl.BlockSpec((CONV_A_WIDTH, W_MIX), const),
                  pl.BlockSpec((1, W_MIX), const),
                  pl.BlockSpec((1, W_MIX), const),
                  pl.BlockSpec((1, W_MIX), const),
                  pl.BlockSpec((CONV_B_WIDTH, W_MIX), const)],
        out_specs=[pl.BlockSpec((None, t, W_MIX), main),
                   pl.BlockSpec((None, t, W_MIX), main)],
        out_shape=[jax.ShapeDtypeStruct((nb, s, W_MIX), BF16)] * 2,
        scratch_shapes=[pltpu.VMEM((t + 2 * HALO, W_MIX), F32),
                        pltpu.VMEM((t + 2 * HALO, W_MIX), F32)],
        compiler_params=_cparams(("parallel", "parallel")),
        name="conv_mixers",
    )(pa, pa, pa, pb, pb, pb, wa, ba.reshape(1, W_MIX), lg.reshape(1, W_MIX),
      lb.reshape(1, W_MIX), wb)


def _head_sumsq(x, ones_bd):
    x2 = x * x
    hi = x2.astype(BF16)
    lo = (x2 - hi.astype(F32)).astype(BF16)
    return (jnp.dot(hi, ones_bd, preferred_element_type=F32)
            + jnp.dot(lo, ones_bd, preferred_element_type=F32))


def _rope(x, cos, sin_signed, half):
    w = x.shape[-1]
    lane = lax.broadcasted_iota(jnp.int32, x.shape, 1)
    fwd = pltpu.roll(x, w - half, 1)
    bwd = pltpu.roll(x, half, 1)
    rot = jnp.where(lane % (2 * half) < half, fwd, bwd)
    return x * cos + rot * sin_signed


def _prep_kernel(pc_ref, pd_ref, gains_ref, cosc_ref, sinc_ref, cosd_ref, sind_ref, ones_ref,
                 c1_ref, c4_ref, c16_ref, qd_ref, kd_ref, vd_ref, qkv_scr):
    c_refs = (c1_ref, c4_ref, c16_ref)
    scale = HEAD_DIM ** -0.5
    ones_bd = ones_ref[...]
    pc = pc_ref[...]
    pd = pd_ref[...]
    gains = gains_ref[...]
    cosc, sinc = cosc_ref[...], sinc_ref[...]
    cosd, sind = cosd_ref[...], sind_ref[...]

    def normed(x, g, ones):
        ss = _head_sumsq(x, ones)
        return x * lax.rsqrt(ss * (1.0 / HEAD_DIM) + EPS) * g

    qc = _rope(normed(pc[:, 0:256], gains[0:1, :], ones_bd), cosc, sinc, HEAD_DIM // 2) * scale
    kc = _rope(normed(pc[:, 256:512], gains[1:2, :], ones_bd), cosc, sinc, HEAD_DIM // 2)
    qkv = (qc, kc, pc[:, 512:768])
    ncb = SZ_C // LANES
    for c in range(ncb):
        part = qkv[c * LANES // W_MIX]
        lo = c * LANES % W_MIX
        qkv_scr[c] = part[:, lo:lo + LANES]
    ts = pc.shape[0]
    for ref, (_, dil) in zip(c_refs, DIL_PATTERNS):
        for r in range(dil):
            for c in range(ncb):
                ref[r, :, c * LANES:(c + 1) * LANES] = (
                    qkv_scr[c, pl.ds(r, ts // dil, stride=dil), :].astype(BF16))

    qd = _rope(normed(pd[:, 0:256], gains[2:3, :], ones_bd), cosd, sind, HEAD_DIM // 4) * (scale * LOG2E)
    kd = _rope(normed(pd[:, 256:384], gains[3:4, 0:128], ones_bd[0:128, 0:128]),
               cosd[:, 0:128], sind[:, 0:128], HEAD_DIM // 4)
    vd = pd[:, 384:512]
    group = N_HEADS // N_KV_HEADS
    for h in range(N_HEADS):
        qd_ref[h // group, h % group] = qd[:, h * HEAD_DIM:(h + 1) * HEAD_DIM].astype(BF16)
    lane = lax.broadcasted_iota(jnp.int32, vd.shape, 1)
    kd_t = kd.T.astype(BF16)
    for h in range(N_KV_HEADS):
        kd_ref[h] = kd_t[h * HEAD_DIM:(h + 1) * HEAD_DIM, :]
        vh = vd if h == 0 else pltpu.roll(vd, LANES - h * HEAD_DIM, 1)
        vd_ref[h] = jnp.where(lane < HEAD_DIM, vh, jnp.where(lane == HEAD_DIM, 1.0, 0.0)).astype(BF16)


def _rope_tables(s):
    def angles(pos, dim):
        inv = ROPE_THETA ** (-jnp.arange(0, dim, 2, dtype=F32) / dim)
        return pos.astype(F32)[:, None] * inv[None, :]

    pos = jnp.arange(s)
    ang = angles(pos, HEAD_DIM)
    cos_c = jnp.tile(jnp.concatenate([jnp.cos(ang), jnp.cos(ang)], -1), (1, N_HEADS))
    sin_c = jnp.tile(jnp.concatenate([-jnp.sin(ang), jnp.sin(ang)], -1), (1, N_HEADS))
    row = pos // GRID_W
    col = pos % GRID_W
    ar = angles(row, HEAD_DIM // 2)
    ac = angles(col, HEAD_DIM // 2)
    cos_d = jnp.tile(jnp.concatenate([jnp.cos(ar), jnp.cos(ar), jnp.cos(ac), jnp.cos(ac)], -1),
                     (1, N_HEADS))
    sin_d = jnp.tile(jnp.concatenate([-jnp.sin(ar), jnp.sin(ar), -jnp.sin(ac), jnp.sin(ac)], -1),
                     (1, N_HEADS))
    return cos_c, sin_c, cos_d, sin_d


def _prep(pc, pd, gains, tables, ts):
    nb, s, _ = pc.shape
    cos_c, sin_c, cos_d, sin_d = tables
    head = lax.broadcasted_iota(jnp.int32, (W_MIX, W_MIX), 0) // HEAD_DIM
    ones_bd = (head == head.T).astype(BF16)
    tok = lambda b, i: (b, i, 0)
    tab = lambda b, i: (i, 0)
    const = lambda b, i: (0, 0)
    hm = lambda b, i: (b, 0, i, 0)
    group = N_HEADS // N_KV_HEADS
    return pl.pallas_call(
        _prep_kernel,
        grid=(nb, s // ts),
        in_specs=[pl.BlockSpec((None, ts, SZ_C), tok),
                  pl.BlockSpec((None, ts, SZ_D), tok),
                  pl.BlockSpec((4, W_MIX), const),
                  pl.BlockSpec((ts, W_MIX), tab),
                  pl.BlockSpec((ts, W_MIX), tab),
                  pl.BlockSpec((ts, W_MIX), tab),
                  pl.BlockSpec((ts, W_MIX), tab),
                  pl.BlockSpec((W_MIX, W_MIX), const)],
        out_specs=[pl.BlockSpec((None, dil, ts // dil, SZ_C), hm) for _, dil in DIL_PATTERNS]
        + [pl.BlockSpec((None, N_KV_HEADS, None, group, ts, HEAD_DIM),
                                lambda b, i: (b, 0, i, 0, 0, 0)),
                   pl.BlockSpec((None, N_KV_HEADS, HEAD_DIM, ts), lambda b, i: (b, 0, 0, i)),
                   pl.BlockSpec((None, N_KV_HEADS, ts, LANES), hm)],
        out_shape=[jax.ShapeDtypeStruct((nb, dil, s // dil, SZ_C), BF16) for _, dil in DIL_PATTERNS]
        + [jax.ShapeDtypeStruct((nb, N_KV_HEADS, s // ts, group, ts, HEAD_DIM), BF16),
           jax.ShapeDtypeStruct((nb, N_KV_HEADS, HEAD_DIM, s), BF16),
           jax.ShapeDtypeStruct((nb, N_KV_HEADS, s, LANES), BF16)],
        scratch_shapes=[pltpu.VMEM((SZ_C // LANES, ts, LANES), F32)],
        compiler_params=_cparams(("parallel", "parallel")),
        name="qk_prep",
    )(pc, pd, gains, cos_c, sin_c, cos_d, sin_d, ones_bd)


def _dilated_kernel(q_ref, k_ref, v_ref, o_ref, lse_ref, *, seq_sub, half, lq):
    span = Q_BLOCK + 2 * half
    qi = pl.program_id(2)
    nt = (((1,), (1,)), ((), ()))
    for i in range(lq // Q_BLOCK):
        q0 = i * Q_BLOCK
        qabs = qi * lq + q0
        ks = pl.multiple_of(jnp.clip(qabs - half, 0, seq_sub - span), half)
        q = q_ref[q0:q0 + Q_BLOCK, :]
        k = k_ref[pl.ds(ks, span), :]
        v = v_ref[pl.ds(ks, span), :]
        qpos = qabs + lax.broadcasted_iota(jnp.int32, (Q_BLOCK, span), 0)
        kpos = ks + lax.broadcasted_iota(jnp.int32, (Q_BLOCK, span), 1)
        mask = jnp.abs(kpos - qpos) <= half
        o_parts, l_parts = [], []
        for h in range(N_HEADS):
            hs = slice(h * HEAD_DIM, (h + 1) * HEAD_DIM)
            sc = lax.dot_general(q[:, hs], k[:, hs], nt, preferred_element_type=F32)
            sc = jnp.where(mask, sc, NEG_INF)
            m = jnp.max(sc, axis=-1, keepdims=True)
            p = jnp.exp(sc - m)
            l = jnp.sum(p, axis=-1, keepdims=True)
            num = jnp.dot(p.astype(BF16), v[:, hs], preferred_element_type=F32)
            o_parts.append(num / l)
            l_parts.append(jnp.broadcast_to(m + jnp.log(l), (Q_BLOCK, HEAD_DIM)))
        o_ref[q0:q0 + Q_BLOCK, :] = jnp.concatenate(o_parts, axis=-1)
        lse_ref[q0:q0 + Q_BLOCK, :] = jnp.concatenate(l_parts, axis=-1)


def _dilated_pattern(qkv, win, dil):
    nb, _, seq_sub, _ = qkv.shape
    half = win // (2 * dil)
    lq = min(seq_sub, 512)
    qspec = pl.BlockSpec((None, None, lq, W_MIX), lambda b, r, i: (b, r, i, 0))
    kspec = pl.BlockSpec((None, None, seq_sub, W_MIX), lambda b, r, i: (b, r, 0, 1))
    vspec = pl.BlockSpec((None, None, seq_sub, W_MIX), lambda b, r, i: (b, r, 0, 2))
    return pl.pallas_call(
        functools.partial(_dilated_kernel, seq_sub=seq_sub, half=half, lq=lq),
        grid=(nb, dil, seq_sub // lq),
        in_specs=[qspec, kspec, vspec],
        out_specs=[qspec, qspec],
        out_shape=[jax.ShapeDtypeStruct((nb, dil, seq_sub, W_MIX), F32)] * 2,
        compiler_params=_cparams(("parallel", "parallel", "arbitrary")),
        name=f"dilated_w{win}_d{dil}",
    )(qkv, qkv, qkv)


def _mixture_kernel(*refs, ts):
    n = len(DIL_PATTERNS)
    o_refs, l_refs = refs[0:2 * n:2], refs[1:2 * n:2]
    uc_ref = refs[2 * n]
    scr = refs[2 * n + 1:]

    def natural(ref, dil, buf):
        if dil == 1:
            return ref[0]
        ncb = W_MIX // LANES
        for r in range(dil):
            for c in range(ncb):
                buf[c, pl.ds(r, ts // dil, stride=dil), :] = ref[r, :, c * LANES:(c + 1) * LANES]
        return jnp.concatenate([buf[c] for c in range(ncb)], axis=-1)

    os_, ls_ = [], []
    for j, (_, dil) in enumerate(DIL_PATTERNS):
        os_.append(natural(o_refs[j], dil, scr[2 * j]))
        ls_.append(natural(l_refs[j], dil, scr[2 * j + 1]))
    mx = functools.reduce(jnp.maximum, ls_)
    ws = [jnp.exp(l - mx) for l in ls_]
    num = sum(w * o for w, o in zip(ws, os_))
    uc_ref[...] = (num / sum(ws)).astype(uc_ref.dtype)


def _dilated_mixture(c_layouts, ts):
    nb, _, s, _ = c_layouts[0].shape
    ins, in_specs = [], []
    for qkv, (win, dil) in zip(c_layouts, DIL_PATTERNS):
        o, lse = _dilated_pattern(qkv, win, dil)
        spec = pl.BlockSpec((None, dil, ts // dil, W_MIX), lambda b, i: (b, 0, i, 0))
        ins += [o, lse]
        in_specs += [spec, spec]
    return pl.pallas_call(
        functools.partial(_mixture_kernel, ts=ts),
        grid=(nb, s // ts),
        in_specs=in_specs,
        out_specs=pl.BlockSpec((None, ts, W_MIX), lambda b, i: (b, i, 0)),
        out_shape=jax.ShapeDtypeStruct((nb, s, W_MIX), BF16),
        scratch_shapes=[pltpu.VMEM((W_MIX // LANES, ts, LANES), F32)] * (2 * len(DIL_PATTERNS)),
        compiler_params=_cparams(("parallel", "parallel")),
        name="dilated_mixture",
    )(*ins)


def _gqa_kernel(q_ref, k_ref, v_ref, o_ref, *, tq, tk, s):
    group = N_HEADS // N_KV_HEADS
    rows = group * tq
    qs = [q_ref[g] for g in range(N_KV_HEADS)]

    def body(c, carry):
        k0 = pl.multiple_of(c * tk, tk)
        new = []
        for g in range(N_KV_HEADS):
            kt = k_ref[g, :, pl.ds(k0, tk)]
            v = v_ref[g, pl.ds(k0, tk), :]
            m, acc = carry[g]
            sc = jnp.dot(qs[g], kt, preferred_element_type=F32)
            m_new = jnp.maximum(m, jnp.max(sc, axis=-1, keepdims=True))
            alpha = jnp.exp2(m - m_new)
            p = jnp.exp2(sc - m_new).astype(BF16)
            acc = alpha * acc + jnp.dot(p, v, preferred_element_type=F32)
            new.append((m_new, acc))
        return tuple(new)

    init = tuple((jnp.full((rows, 1), -jnp.inf, F32), jnp.zeros((rows, LANES), F32))
                 for _ in range(N_KV_HEADS))
    fin = lax.fori_loop(0, s // tk, body, init)
    outs = []
    for (_, acc) in fin:
        o = acc[:, :HEAD_DIM] / acc[:, HEAD_DIM:HEAD_DIM + 1]
        outs += [o[j * tq:(j + 1) * tq, :] for j in range(group)]
    o_ref[...] = jnp.concatenate(outs, axis=-1).astype(o_ref.dtype)


def _gqa(qd, kd, vd, tq, tk):
    nb, _, nq, group, _, _ = qd.shape
    s = nq * tq
    qd = qd.reshape(nb, N_KV_HEADS, nq, group * tq, HEAD_DIM)
    return pl.pallas_call(
        functools.partial(_gqa_kernel, tq=tq, tk=tk, s=s),
        grid=(nb, nq),
        in_specs=[pl.BlockSpec((None, N_KV_HEADS, None, group * tq, HEAD_DIM),
                               lambda b, i: (b, 0, i, 0, 0)),
                  pl.BlockSpec((None, N_KV_HEADS, HEAD_DIM, s), lambda b, i: (b, 0, 0, 0)),
                  pl.BlockSpec((None, N_KV_HEADS, s, LANES), lambda b, i: (b, 0, 0, 0))],
        out_specs=pl.BlockSpec((None, tq, W_MIX), lambda b, i: (b, i, 0)),
        out_shape=jax.ShapeDtypeStruct((nb, s, W_MIX), BF16),
        compiler_params=_cparams(("parallel", "arbitrary")),
        name="gqa",
    )(qd, kd, vd)


def _outproj_kernel(ua_ref, ub_ref, uc_ref, ud_ref, x_ref, w_ref, g_ref, wr_ref, br_ref,
                    x1_ref, xe_ref):
    acc = x_ref[...]
    for j, ref in enumerate((ua_ref, ub_ref, uc_ref, ud_ref)):
        acc = acc + jnp.dot(ref[...], w_ref[j * W_MIX:(j + 1) * W_MIX, :],
                            preferred_element_type=F32)
    x1_ref[...] = acc
    ms = jnp.mean(acc * acc, axis=-1, keepdims=True)
    xn = acc * lax.rsqrt(ms + EPS) * g_ref[...]
    logits = jnp.dot(xn.astype(BF16), wr_ref[...], preferred_element_type=F32) + br_ref[...]
    xe_ref[:, :D_MODEL] = xn
    xe_ref[:, D_MODEL:] = _route(logits)


def _route(logits):
    lane = lax.broadcasted_iota(jnp.int32, logits.shape, 1)
    big = jnp.int32(LANES)
    ninf = -jnp.inf

    def argmax(vals):
        mx = jnp.max(vals, axis=-1, keepdims=True)
        idx = jnp.min(jnp.where(vals == mx, lane, big), axis=-1, keepdims=True)
        return mx, idx

    glog = jnp.where(lane < N_GROUPS, logits, ninf)
    gmax, gidx = argmax(glog)
    g_w = 1.0 / jnp.sum(jnp.exp(glog - gmax), axis=-1, keepdims=True)
    lo = N_GROUPS + gidx * EXPERTS_PER_GROUP
    el = jnp.where((lane >= lo) & (lane < lo + EXPERTS_PER_GROUP), logits, ninf)
    t1, i1 = argmax(el)
    el2 = jnp.where(lane == i1, ninf, el)
    t2, i2 = argmax(el2)
    e2 = jnp.exp(t2 - t1)
    w1 = 1.0 / (1.0 + e2)
    w2 = e2 * w1
    gates = jnp.where(lane == i1, w1 * g_w, jnp.where(lane == i2, w2 * g_w, 0.0))
    return jnp.where(lane == 0, gidx.astype(F32), gates)


def _outproj(ua, ub, uc, ud, x2, w_bf16, g2, wr, br, tm):
    n = x2.shape[0]
    row = lambda i: (i, 0)
    const = lambda i: (0, 0)
    return pl.pallas_call(
        _outproj_kernel,
        grid=(n // tm,),
        in_specs=[pl.BlockSpec((tm, W_MIX), row)] * 4
        + [pl.BlockSpec((tm, D_MODEL), row),
           pl.BlockSpec((D_MODEL, D_MODEL), const),
           pl.BlockSpec((1, D_MODEL), const),
           pl.BlockSpec((D_MODEL, LANES), const),
           pl.BlockSpec((1, LANES), const)],
        out_specs=[pl.BlockSpec((tm, D_MODEL), row),
                   pl.BlockSpec((tm, D_ROW), row)],
        out_shape=[jax.ShapeDtypeStruct((n, D_MODEL), F32),
                   jax.ShapeDtypeStruct((n, D_ROW), F32)],
        compiler_params=_cparams(("parallel",)),
        name="outproj_router",
    )(ua, ub, uc, ud, x2, w_bf16, g2.reshape(1, D_MODEL), wr, br)


def _moe_plan(gidx, tm):
    n = gidx.shape[0]
    nt = n // tm + N_GROUPS
    onehot = (gidx[:, None] == jnp.arange(N_GROUPS, dtype=jnp.int32)[None, :]).astype(jnp.int32)
    csum = jnp.cumsum(onehot, axis=0)
    counts = csum[-1]
    rank = jnp.sum(csum * onehot, axis=1) - 1
    gtiles = (counts + tm - 1) // tm
    tstart = jnp.cumsum(gtiles) - gtiles
    dest = tstart[gidx] * tm + rank
    src = jnp.full((nt * tm,), -1, jnp.int32).at[dest].set(jnp.arange(n, dtype=jnp.int32))
    t = jnp.arange(nt, dtype=jnp.int32)
    tgroup = jnp.clip(jnp.sum((t[:, None] >= tstart[None, :]).astype(jnp.int32), axis=1) - 1,
                      0, N_GROUPS - 1)
    p = jnp.arange(nt * tm, dtype=jnp.int32)
    trash = n + p % (2 * tm)
    src_g = jnp.where(src < 0, 0, src).reshape(nt, 1, tm)
    src_s = jnp.where(src < 0, trash, src).reshape(nt, 1, tm)
    return tgroup, src_g, src_s


def _moe_kernel(tg_ref, srcg_cur_ref, srcg_nxt_ref, srcs_ref, xe_hbm, wg_ref, wu_ref, wd_ref,
                y_hbm, xg, ys, gsem, ssem, *, tm, n):
    t = pl.program_id(0)
    nt = pl.num_programs(0)
    slot = t % 2

    def row_gather(src_ref, dst_slot):
        def body(r, c):
            tok = src_ref[0, r]
            pltpu.make_async_copy(xe_hbm.at[tok], xg.at[dst_slot, r], gsem.at[dst_slot]).start()
            return c
        lax.fori_loop(0, tm, body, 0, unroll=True)

    def tile_gather_wait(s_):
        pltpu.make_async_copy(xe_hbm.at[pl.ds(0, tm)], xg.at[s_], gsem.at[s_]).wait()

    def tile_scatter_wait(s_):
        pltpu.make_async_copy(ys.at[s_], y_hbm.at[pl.ds(0, tm)], ssem.at[s_]).wait()

    @pl.when(t == 0)
    def _():
        row_gather(srcg_cur_ref, 0)
        ys[1] = jnp.zeros((tm, D_MODEL), F32)
        for half in range(2):
            cp = pltpu.make_async_copy(ys.at[1], y_hbm.at[pl.ds(n + half * tm, tm)], ssem.at[1])
            cp.start()
            cp.wait()

    tile_gather_wait(slot)

    @pl.when(t >= 2)
    def _():
        tile_scatter_wait(slot)

    row_gather(srcg_nxt_ref, 1 - slot)

    g = tg_ref[t]
    xe = xg[slot]
    x = xe[:, :D_MODEL].astype(BF16)
    gates = xe[:, D_MODEL:]
    lane = lax.broadcasted_iota(jnp.int32, gates.shape, 1)
    first = N_GROUPS + g * EXPERTS_PER_GROUP
    y = jnp.zeros((tm, D_MODEL), F32)
    for j in range(EXPERTS_PER_GROUP):
        gj = jnp.sum(jnp.where(lane == first + j, gates, 0.0), axis=-1, keepdims=True)
        a = jnp.dot(x, wg_ref[j], preferred_element_type=F32)
        u = jnp.dot(x, wu_ref[j], preferred_element_type=F32)
        hdn = (a * jax.nn.sigmoid(a) * u).astype(BF16)
        y = y + gj * jnp.dot(hdn, wd_ref[j], preferred_element_type=F32)
    ys[slot] = y

    def scat(r, c):
        tok = srcs_ref[0, r]
        pltpu.make_async_copy(ys.at[slot, r], y_hbm.at[tok], ssem.at[slot]).start()
        return c
    lax.fori_loop(0, tm, scat, 0, unroll=True)

    @pl.when(t == nt - 1)
    def _():
        tile_gather_wait(1 - slot)
        tile_scatter_wait(slot)

        @pl.when(nt >= 2)
        def _():
            tile_scatter_wait(1 - slot)


def _moe(xe, wg, wu, wd, tm):
    n = xe.shape[0]
    gidx = xe[:, D_MODEL].astype(jnp.int32)
    tgroup, src_g, src_s = _moe_plan(gidx, tm)
    nt = tgroup.shape[0]
    grp = lambda a: a.reshape((N_GROUPS, EXPERTS_PER_GROUP) + a.shape[1:])
    smem = lambda imap: pl.BlockSpec((None, 1, tm), imap, memory_space=pltpu.SMEM)
    wspec = lambda shp: pl.BlockSpec((None, EXPERTS_PER_GROUP) + shp, lambda t, tg: (tg[t], 0, 0, 0))
    grid_spec = pltpu.PrefetchScalarGridSpec(
        num_scalar_prefetch=1,
        grid=(nt,),
        in_specs=[smem(lambda t, tg: (t, 0, 0)),
                  smem(lambda t, tg: (jnp.minimum(t + 1, nt - 1), 0, 0)),
                  smem(lambda t, tg: (t, 0, 0)),
                  pl.BlockSpec(memory_space=pl.ANY),
                  wspec((D_MODEL, D_EXPERT)),
                  wspec((D_MODEL, D_EXPERT)),
                  wspec((D_EXPERT, D_MODEL))],
        out_specs=pl.BlockSpec(memory_space=pl.ANY),
        scratch_shapes=[pltpu.VMEM((2, tm, D_ROW), F32),
                        pltpu.VMEM((2, tm, D_MODEL), F32),
                        pltpu.SemaphoreType.DMA((2,)),
                        pltpu.SemaphoreType.DMA((2,))])
    return pl.pallas_call(
        functools.partial(_moe_kernel, tm=tm, n=n),
        grid_spec=grid_spec,
        out_shape=jax.ShapeDtypeStruct((n + 2 * tm, D_MODEL), F32),
        compiler_params=_cparams(("arbitrary",)),
        name="moe",
    )(tgroup, src_g, src_g, src_s, xe, grp(wg), grp(wu), grp(wd))


def _residual_kernel(x1_ref, y_ref, o_ref):
    o_ref[...] = x1_ref[...] + y_ref[...]


def _residual(x1, y, tm):
    n = x1.shape[0]
    row = lambda i: (i, 0)
    return pl.pallas_call(
        _residual_kernel,
        grid=(n // tm,),
        in_specs=[pl.BlockSpec((tm, D_MODEL), row), pl.BlockSpec((tm, D_MODEL), row)],
        out_specs=pl.BlockSpec((tm, D_MODEL), row),
        out_shape=jax.ShapeDtypeStruct((n, D_MODEL), F32),
        compiler_params=_cparams(("parallel",)),
        name="moe_residual",
    )(x1, y)


def _tiles(nb, s):
    return dict(tm=min(512, s), tconv=min(512, s),
                tq=min(512, s), tk=min(2048, s), tmoe=min(512, s))


def _layer(x, p, l, tables, tl):
    nb, s, _ = x.shape
    n = nb * s
    x2 = x.reshape(n, D_MODEL)
    pa, pb, pc, pd = _inproj(x2, p["norm1_g"][l], p["w_in"][l], tl["tm"])
    pa = pa.reshape(nb, s, SZ_A)
    pb = pb.reshape(nb, s, SZ_B)
    pc = pc.reshape(nb, s, SZ_C)
    pd = pd.reshape(nb, s, SZ_D)
    ua, ub = _conv_mixers(pa, pb, p["conv_a_w"][l], p["conv_a_b"][l], p["ln_a_g"][l],
                          p["ln_a_b"][l], p["conv_b_w"][l], tl["tconv"])
    gains = jnp.stack([jnp.tile(p[k][l], N_HEADS) for k in ("qn_c", "kn_c", "qn_d", "kn_d")])
    c1, c4, c16, qd, kd, vd = _prep(pc, pd, gains, tables, tl["tq"])
    uc = _dilated_mixture((c1, c4, c16), tl["tm"])
    ud = _gqa(qd, kd, vd, tl["tq"], tl["tk"])
    x1, xe = _outproj(ua.reshape(n, W_MIX), ub.reshape(n, W_MIX), uc.reshape(n, W_MIX),
                      ud.reshape(n, W_MIX), x2, p["w_out"][l], p["norm2_g"][l],
                      p["w_route"][l], p["b_route"][l], tl["tm"])
    y = _moe(xe, p["w_gate"][l], p["w_up"][l], p["w_down"][l], tl["tmoe"])
    return _residual(x1, y, tl["tm"]).reshape(nb, s, D_MODEL)


def _trunk(x, p):
    nb, s, _ = x.shape
    tables = _rope_tables(s)
    tl = _tiles(nb, s)
    for l in range(p["w_in"].shape[0]):
        x = _layer(x, p, l, tables, tl)
    return x


def _pack_params(norm1_g, w_in, conv_a_w, conv_a_b, ln_a_g, ln_a_b, conv_b_w, qn_c, kn_c, qn_d, kn_d,
                 w_out, norm2_g, w_group, b_group, w_router, b_router, w_gate, w_up, w_down):
    depth = w_in.shape[0]
    pad = LANES - N_GROUPS - N_EXPERTS
    w_route = jnp.concatenate([w_group, w_router, jnp.zeros((depth, D_MODEL, pad), F32)], axis=-1)
    b_route = jnp.concatenate([b_group, b_router, jnp.zeros((depth, pad), F32)], axis=-1)
    return dict(norm1_g=norm1_g, w_in=w_in.astype(BF16), conv_a_w=conv_a_w, conv_a_b=conv_a_b,
                ln_a_g=ln_a_g, ln_a_b=ln_a_b, conv_b_w=conv_b_w, qn_c=qn_c, kn_c=kn_c, qn_d=qn_d,
                kn_d=kn_d, w_out=w_out.astype(BF16), norm2_g=norm2_g,
                w_route=w_route.astype(BF16), b_route=b_route.reshape(depth, 1, LANES),
                w_gate=w_gate.astype(BF16), w_up=w_up.astype(BF16), w_down=w_down.astype(BF16))


def kernel(x_prompt, x_sample, norm1_g, w_in, conv_a_w, conv_a_b, ln_a_g, ln_a_b, conv_b_w, qn_c, kn_c, qn_d, kn_d, w_out, norm2_g, w_group, b_group, w_router, b_router, w_gate, w_up, w_down):
    p = _pack_params(norm1_g, w_in, conv_a_w, conv_a_b, ln_a_g, ln_a_b, conv_b_w, qn_c, kn_c, qn_d,
                     kn_d, w_out, norm2_g, w_group, b_group, w_router, b_router, w_gate, w_up, w_down)
    assert x_prompt.shape[1:] == x_sample.shape[1:]
    nbp = x_prompt.shape[0]
    y = _trunk(jnp.concatenate([x_prompt, x_sample], axis=0), p)
    return (y[:nbp], y[nbp:])
```

```python
import functools
import math

import jax
import jax.numpy as jnp
from jax import lax
from jax.experimental import pallas as pl
from jax.experimental.pallas import tpu as pltpu

D_MODEL = 1024
HEAD_DIM = 64
W_MIX = 256
N_HEADS = 4
N_KV_HEADS = 2
CONV_A_WIDTH = 31
CONV_B_WIDTH = 3
DIL_PATTERNS = ((128, 1), (512, 4), (2048, 16))
Q_BLOCK = 128
GRID_W = 64
ROPE_THETA = 10000.0
N_GROUPS = 4
EXPERTS_PER_GROUP = 4
N_EXPERTS = 16
D_EXPERT = 512
EPS = 1e-6
NEG_INF = -1e30
LOG2E = math.log2(math.e)

SZ_A, SZ_B, SZ_C, SZ_D = 512, 768, 768, 512
D_IN = SZ_A + SZ_B + SZ_C + SZ_D
LANES = 128
SUBLANES = 8
D_ROW = D_MODEL + LANES
HALO = 16
VMEM_LIMIT = 56 * 1024 * 1024

F32 = jnp.float32
BF16 = jnp.bfloat16


def _cparams(sem):
    return pltpu.CompilerParams(dimension_semantics=sem, vmem_limit_bytes=VMEM_LIMIT)


def _tile_of_source(i, *_, start, count):
    return (jnp.clip(i - start, 0, count - 1), 0)


def _row_source_specs(srcs, tm):
    specs, starts, start = [], [], 0
    for a in srcs:
        count = a.shape[0] // tm
        specs.append(pl.BlockSpec((tm, a.shape[1]),
                                  functools.partial(_tile_of_source, start=start, count=count)))
        starts.append(start)
        start += count
    return specs, tuple(starts), start


def _pick_source(refs, starts, i):
    x = refs[0][...]
    for ref, start in zip(refs[1:], starts[1:]):
        x = jnp.where(i >= start, ref[...], x)
    return x


def _inproj_kernel(*refs, starts):
    ns = len(starts)
    g_ref, w_ref, pa_ref, pb_ref, pc_ref, pd_ref = refs[ns:]
    x = _pick_source(refs[:ns], starts, pl.program_id(0))
    ms = jnp.mean(x * x, axis=-1, keepdims=True)
    h = (x * lax.rsqrt(ms + EPS) * g_ref[...]).astype(BF16)
    off = 0
    for ref, sz in ((pa_ref, SZ_A), (pb_ref, SZ_B), (pc_ref, SZ_C), (pd_ref, SZ_D)):
        ref[...] = jnp.dot(h, w_ref[:, off:off + sz], preferred_element_type=F32)
        off += sz


def _inproj(xsrcs, g, w_bf16, tm):
    xspecs, starts, ntiles = _row_source_specs(xsrcs, tm)
    n = ntiles * tm
    outs = [jax.ShapeDtypeStruct((n, sz), F32) for sz in (SZ_A, SZ_B, SZ_C, SZ_D)]
    return pl.pallas_call(
        functools.partial(_inproj_kernel, starts=starts),
        grid=(ntiles,),
        in_specs=xspecs + [pl.BlockSpec((1, D_MODEL), lambda i: (0, 0)),
                           pl.BlockSpec((D_MODEL, D_IN), lambda i: (0, 0))],
        out_specs=[pl.BlockSpec((tm, sz), lambda i: (i, 0)) for sz in (SZ_A, SZ_B, SZ_C, SZ_D)],
        out_shape=outs,
        compiler_params=_cparams(("parallel",)),
        name="inproj",
    )(*xsrcs, g.reshape(1, D_MODEL), w_bf16)


def _conv_kernel(pa_ref, pa_prev_ref, pa_next_ref, pb_ref, pb_prev_ref, pb_next_ref,
                 wa_ref, ba_ref, lg_ref, lb_ref, wb_ref, ua_ref, ub_ref, bufa, bufb, *, t):
    si = pl.program_id(1)
    ns = pl.num_programs(1)
    has_prev = (si > 0).astype(F32)
    has_next = (si < ns - 1).astype(F32)

    def glu(ref):
        p = ref[...]
        return p[:, :W_MIX] * jax.nn.sigmoid(p[:, W_MIX:])

    def gated(ref):
        p = ref[...]
        return p[:, W_MIX:2 * W_MIX] * p[:, 2 * W_MIX:]

    bufa[0:HALO, :] = glu(pa_prev_ref) * has_prev
    bufa[HALO:HALO + t, :] = glu(pa_ref)
    bufa[HALO + t:2 * HALO + t, :] = glu(pa_next_ref) * has_next
    bufb[0:HALO, :] = gated(pb_prev_ref) * has_prev
    bufb[HALO:HALO + t, :] = gated(pb_ref)
    bufb[HALO + t:2 * HALO + t, :] = gated(pb_next_ref) * has_next

    first = HALO - CONV_A_WIDTH // 2
    acc = jnp.zeros((t, W_MIX), F32)
    for b in range(SUBLANES):
        z = None
        for a in range((2 * HALO) // SUBLANES):
            k = SUBLANES * a + b - first
            if 0 <= k < CONV_A_WIDTH:
                term = bufa[SUBLANES * a:SUBLANES * a + t + SUBLANES, :] * wa_ref[k:k + 1, :]
                z = term if z is None else z + term
        acc = acc + z[b:b + t, :]
    acc = acc + ba_ref[...]
    mu = jnp.mean(acc, axis=-1, keepdims=True)
    xc = acc - mu
    var = jnp.mean(xc * xc, axis=-1, keepdims=True)
    y = xc * lax.rsqrt(var + EPS) * lg_ref[...] + lb_ref[...]
    ua_ref[...] = (y * jax.nn.sigmoid(y)).astype(BF16)

    accb = jnp.zeros((t, W_MIX), F32)
    for k in range(CONV_B_WIDTH):
        start = HALO - CONV_B_WIDTH // 2 + k
        accb = accb + bufb[start:start + t, :] * wb_ref[k:k + 1, :]
    ub_ref[...] = (pb_ref[:, :W_MIX] * accb).astype(BF16)


def _conv_mixers(pa, pb, wa, ba, lg, lb, wb, t):
    nb, s, _ = pa.shape
    hb = t // HALO
    last = s // HALO - 1

    def main(b, i):
        return (b, i, 0)

    def prev(b, i):
        return (b, jnp.maximum(i * hb - 1, 0), 0)

    def nxt(b, i):
        return (b, jnp.minimum((i + 1) * hb, last), 0)

    const = lambda b, i: (0, 0)
    return pl.pallas_call(
        functools.partial(_conv_kernel, t=t),
        grid=(nb, s // t),
        in_specs=[pl.BlockSpec((None, t, SZ_A), main),
                  pl.BlockSpec((None, HALO, SZ_A), prev),
                  pl.BlockSpec((None, HALO, SZ_A), nxt),
                  pl.BlockSpec((None, t, SZ_B), main),
                  pl.BlockSpec((None, HALO, SZ_B), prev),
                  pl.BlockSpec((None, HALO, SZ_B), nxt),
                  pl.BlockSpec((CONV_A_WIDTH, W_MIX), const),
                  pl.BlockSpec((1, W_MIX), const),
                  pl.BlockSpec((1, W_MIX), const),
                  pl.BlockSpec((1, W_MIX), const),
                  pl.BlockSpec((CONV_B_WIDTH, W_MIX), const)],
        out_specs=[pl.BlockSpec((None, t, W_MIX), main),
                   pl.BlockSpec((None, t, W_MIX), main)],
        out_shape=[jax.ShapeDtypeStruct((nb, s, W_MIX), BF16)] * 2,
        scratch_shapes=[pltpu.VMEM((t + 2 * HALO, W_MIX), F32),
                        pltpu.VMEM((t + 2 * HALO, W_MIX), F32)],
        compiler_params=_cparams(("parallel", "parallel")),
        name="conv_mixers",
    )(pa, pa, pa, pb, pb, pb, wa, ba.reshape(1, W_MIX), lg.reshape(1, W_MIX),
      lb.reshape(1, W_MIX), wb)


def _head_sumsq(x, ones_bd):
    x2 = x * x
    hi = x2.astype(BF16)
    lo = (x2 - hi.astype(F32)).astype(BF16)
    return (jnp.dot(hi, ones_bd, preferred_element_type=F32)
            + jnp.dot(lo, ones_bd, preferred_element_type=F32))


def _rope(x, cos, sin_signed, half):
    w = x.shape[-1]
    lane = lax.broadcasted_iota(jnp.int32, x.shape, 1)
    fwd = pltpu.roll(x, w - half, 1)
    bwd = pltpu.roll(x, half, 1)
    rot = jnp.where(lane % (2 * half) < half, fwd, bwd)
    return x * cos + rot * sin_signed


def _prep_kernel(pc_ref, pd_ref, gains_ref, cosc_ref, sinc_ref, cosd_ref, sind_ref, ones_ref,
                 c1_ref, c4_ref, c16_ref, qd_ref, kd_ref, vd_ref, qkv_scr):
    c_refs = (c1_ref, c4_ref, c16_ref)
    scale = HEAD_DIM ** -0.5
    ones_bd = ones_ref[...]
    pc = pc_ref[...]
    pd = pd_ref[...]
    gains = gains_ref[...]
    cosc, sinc = cosc_ref[...], sinc_ref[...]
    cosd, sind = cosd_ref[...], sind_ref[...]

    def normed(x, g, ones):
        ss = _head_sumsq(x, ones)
        return x * lax.rsqrt(ss * (1.0 / HEAD_DIM) + EPS) * g

    qc = _rope(normed(pc[:, 0:256], gains[0:1, :], ones_bd), cosc, sinc, HEAD_DIM // 2) * scale
    kc = _rope(normed(pc[:, 256:512], gains[1:2, :], ones_bd), cosc, sinc, HEAD_DIM // 2)
    qkv = (qc, kc, pc[:, 512:768])
    ncb = SZ_C // LANES
    for c in range(ncb):
        part = qkv[c * LANES // W_MIX]
        lo = c * LANES % W_MIX
        qkv_scr[c] = part[:, lo:lo + LANES]
    ts = pc.shape[0]
    for ref, (_, dil) in zip(c_refs, DIL_PATTERNS):
        for r in range(dil):
            for c in range(ncb):
                ref[r, :, c * LANES:(c + 1) * LANES] = (
                    qkv_scr[c, pl.ds(r, ts // dil, stride=dil), :].astype(BF16))

    qd = _rope(normed(pd[:, 0:256], gains[2:3, :], ones_bd), cosd, sind, HEAD_DIM // 4) * (scale * LOG2E)
    kd = _rope(normed(pd[:, 256:384], gains[3:4, 0:128], ones_bd[0:128, 0:128]),
               cosd[:, 0:128], sind[:, 0:128], HEAD_DIM // 4)
    vd = pd[:, 384:512]
    group = N_HEADS // N_KV_HEADS
    for h in range(N_HEADS):
        qd_ref[h // group, h % group] = qd[:, h * HEAD_DIM:(h + 1) * HEAD_DIM].astype(BF16)
    lane = lax.broadcasted_iota(jnp.int32, vd.shape, 1)
    kd_t = kd.T.astype(BF16)
    for h in range(N_KV_HEADS):
        kd_ref[h] = kd_t[h * HEAD_DIM:(h + 1) * HEAD_DIM, :]
        vh = vd if h == 0 else pltpu.roll(vd, LANES - h * HEAD_DIM, 1)
        vd_ref[h] = jnp.where(lane < HEAD_DIM, vh, jnp.where(lane == HEAD_DIM, 1.0, 0.0)).astype(BF16)


def _rope_tables(s):
    def angles(pos, dim):
        inv = ROPE_THETA ** (-jnp.arange(0, dim, 2, dtype=F32) / dim)
        return pos.astype(F32)[:, None] * inv[None, :]

    pos = jnp.arange(s)
    ang = angles(pos, HEAD_DIM)
    cos_c = jnp.tile(jnp.concatenate([jnp.cos(ang), jnp.cos(ang)], -1), (1, N_HEADS))
    sin_c = jnp.tile(jnp.concatenate([-jnp.sin(ang), jnp.sin(ang)], -1), (1, N_HEADS))
    row = pos // GRID_W
    col = pos % GRID_W
    ar = angles(row, HEAD_DIM // 2)
    ac = angles(col, HEAD_DIM // 2)
    cos_d = jnp.tile(jnp.concatenate([jnp.cos(ar), jnp.cos(ar), jnp.cos(ac), jnp.cos(ac)], -1),
                     (1, N_HEADS))
    sin_d = jnp.tile(jnp.concatenate([-jnp.sin(ar), jnp.sin(ar), -jnp.sin(ac), jnp.sin(ac)], -1),
                     (1, N_HEADS))
    return cos_c, sin_c, cos_d, sin_d


def _prep(pc, pd, gains, tables, ts):
    nb, s, _ = pc.shape
    cos_c, sin_c, cos_d, sin_d = tables
    head = lax.broadcasted_iota(jnp.int32, (W_MIX, W_MIX), 0) // HEAD_DIM
    ones_bd = (head == head.T).astype(BF16)
    tok = lambda b, i: (b, i, 0)
    tab = lambda b, i: (i, 0)
    const = lambda b, i: (0, 0)
    hm = lambda b, i: (b, 0, i, 0)
    group = N_HEADS // N_KV_HEADS
    return pl.pallas_call(
        _prep_kernel,
        grid=(nb, s // ts),
        in_specs=[pl.BlockSpec((None, ts, SZ_C), tok),
                  pl.BlockSpec((None, ts, SZ_D), tok),
                  pl.BlockSpec((4, W_MIX), const),
                  pl.BlockSpec((ts, W_MIX), tab),
                  pl.BlockSpec((ts, W_MIX), tab),
                  pl.BlockSpec((ts, W_MIX), tab),
                  pl.BlockSpec((ts, W_MIX), tab),
                  pl.BlockSpec((W_MIX, W_MIX), const)],
        out_specs=[pl.BlockSpec((None, dil, ts // dil, SZ_C), hm) for _, dil in DIL_PATTERNS]
        + [pl.BlockSpec((None, N_KV_HEADS, None, group, ts, HEAD_DIM),
                                lambda b, i: (b, 0, i, 0, 0, 0)),
                   pl.BlockSpec((None, N_KV_HEADS, HEAD_DIM, ts), lambda b, i: (b, 0, 0, i)),
                   pl.BlockSpec((None, N_KV_HEADS, ts, LANES), hm)],
        out_shape=[jax.ShapeDtypeStruct((nb, dil, s // dil, SZ_C), BF16) for _, dil in DIL_PATTERNS]
        + [jax.ShapeDtypeStruct((nb, N_KV_HEADS, s // ts, group, ts, HEAD_DIM), BF16),
           jax.ShapeDtypeStruct((nb, N_KV_HEADS, HEAD_DIM, s), BF16),
           jax.ShapeDtypeStruct((nb, N_KV_HEADS, s, LANES), BF16)],
        scratch_shapes=[pltpu.VMEM((SZ_C // LANES, ts, LANES), F32)],
        compiler_params=_cparams(("parallel", "parallel")),
        name="qk_prep",
    )(pc, pd, gains, cos_c, sin_c, cos_d, sin_d, ones_bd)


def _dilated_kernel(q_ref, k_ref, v_ref, o_ref, lse_ref, *, seq_sub, half, lq):
    span = Q_BLOCK + 2 * half
    qi = pl.program_id(2)
    nt = (((1,), (1,)), ((), ()))
    for i in range(lq // Q_BLOCK):
        q0 = i * Q_BLOCK
        qabs = qi * lq + q0
        ks = pl.multiple_of(jnp.clip(qabs - half, 0, seq_sub - span), half)
        q = q_ref[q0:q0 + Q_BLOCK, :]
        k = k_ref[pl.ds(ks, span), :]
        v = v_ref[pl.ds(ks, span), :]
        qpos = qabs + lax.broadcasted_iota(jnp.int32, (Q_BLOCK, span), 0)
        kpos = ks + lax.broadcasted_iota(jnp.int32, (Q_BLOCK, span), 1)
        mask = jnp.abs(kpos - qpos) <= half
        o_parts, l_parts = [], []
        for h in range(N_HEADS):
            hs = slice(h * HEAD_DIM, (h + 1) * HEAD_DIM)
            sc = lax.dot_general(q[:, hs], k[:, hs], nt, preferred_element_type=F32)
            sc = jnp.where(mask, sc, NEG_INF)
            m = jnp.max(sc, axis=-1, keepdims=True)
            p = jnp.exp(sc - m)
            l = jnp.sum(p, axis=-1, keepdims=True)
            num = jnp.dot(p.astype(BF16), v[:, hs], preferred_element_type=F32)
            o_parts.append(num / l)
            l_parts.append(jnp.broadcast_to(m + jnp.log(l), (Q_BLOCK, HEAD_DIM)))
        o_ref[q0:q0 + Q_BLOCK, :] = jnp.concatenate(o_parts, axis=-1)
        lse_ref[q0:q0 + Q_BLOCK, :] = jnp.concatenate(l_parts, axis=-1)


def _dilated_pattern(qkv, win, dil):
    nb, _, seq_sub, _ = qkv.shape
    half = win // (2 * dil)
    lq = min(seq_sub, 512)
    qspec = pl.BlockSpec((None, None, lq, W_MIX), lambda b, r, i: (b, r, i, 0))
    kspec = pl.BlockSpec((None, None, seq_sub, W_MIX), lambda b, r, i: (b, r, 0, 1))
    vspec = pl.BlockSpec((None, None, seq_sub, W_MIX), lambda b, r, i: (b, r, 0, 2))
    return pl.pallas_call(
        functools.partial(_dilated_kernel, seq_sub=seq_sub, half=half, lq=lq),
        grid=(nb, dil, seq_sub // lq),
        in_specs=[qspec, kspec, vspec],
        out_specs=[qspec, qspec],
        out_shape=[jax.ShapeDtypeStruct((nb, dil, seq_sub, W_MIX), F32)] * 2,
        compiler_params=_cparams(("parallel", "parallel", "arbitrary")),
        name=f"dilated_w{win}_d{dil}",
    )(qkv, qkv, qkv)


def _mixture_kernel(*refs, ts):
    n = len(DIL_PATTERNS)
    o_refs, l_refs = refs[0:2 * n:2], refs[1:2 * n:2]
    uc_ref = refs[2 * n]
    scr = refs[2 * n + 1:]

    def natural(ref, dil, buf):
        if dil == 1:
            return ref[0]
        ncb = W_MIX // LANES
        for r in range(dil):
            for c in range(ncb):
                buf[c, pl.ds(r, ts // dil, stride=dil), :] = ref[r, :, c * LANES:(c + 1) * LANES]
        return jnp.concatenate([buf[c] for c in range(ncb)], axis=-1)

    os_, ls_ = [], []
    for j, (_, dil) in enumerate(DIL_PATTERNS):
        os_.append(natural(o_refs[j], dil, scr[2 * j]))
        ls_.append(natural(l_refs[j], dil, scr[2 * j + 1]))
    mx = functools.reduce(jnp.maximum, ls_)
    ws = [jnp.exp(l - mx) for l in ls_]
    num = sum(w * o for w, o in zip(ws, os_))
    uc_ref[...] = (num / sum(ws)).astype(uc_ref.dtype)


def _dilated_mixture(c_layouts, ts):
    nb, _, s, _ = c_layouts[0].shape
    ins, in_specs = [], []
    for qkv, (win, dil) in zip(c_layouts, DIL_PATTERNS):
        o, lse = _dilated_pattern(qkv, win, dil)
        spec = pl.BlockSpec((None, dil, ts // dil, W_MIX), lambda b, i: (b, 0, i, 0))
        ins += [o, lse]
        in_specs += [spec, spec]
    return pl.pallas_call(
        functools.partial(_mixture_kernel, ts=ts),
        grid=(nb, s // ts),
        in_specs=in_specs,
        out_specs=pl.BlockSpec((None, ts, W_MIX), lambda b, i: (b, i, 0)),
        out_shape=jax.ShapeDtypeStruct((nb, s, W_MIX), BF16),
        scratch_shapes=[pltpu.VMEM((W_MIX // LANES, ts, LANES), F32)] * (2 * len(DIL_PATTERNS)),
        compiler_params=_cparams(("parallel", "parallel")),
        name="dilated_mixture",
    )(*ins)


def _gqa_kernel(q_ref, k_ref, v_ref, o_ref, *, tq, tk, s):
    group = N_HEADS // N_KV_HEADS
    rows = group * tq
    qs = [q_ref[g] for g in range(N_KV_HEADS)]

    def body(c, carry):
        k0 = pl.multiple_of(c * tk, tk)
        new = []
        for g in range(N_KV_HEADS):
            kt = k_ref[g, :, pl.ds(k0, tk)]
            v = v_ref[g, pl.ds(k0, tk), :]
            m, acc = carry[g]
            sc = jnp.dot(qs[g], kt, preferred_element_type=F32)
            m_new = jnp.maximum(m, jnp.max(sc, axis=-1, keepdims=True))
            alpha = jnp.exp2(m - m_new)
            p = jnp.exp2(sc - m_new).astype(BF16)
            acc = alpha * acc + jnp.dot(p, v, preferred_element_type=F32)
            new.append((m_new, acc))
        return tuple(new)

    init = tuple((jnp.full((rows, 1), -jnp.inf, F32), jnp.zeros((rows, LANES), F32))
                 for _ in range(N_KV_HEADS))
    fin = lax.fori_loop(0, s // tk, body, init)
    outs = []
    for (_, acc) in fin:
        o = acc[:, :HEAD_DIM] / acc[:, HEAD_DIM:HEAD_DIM + 1]
        outs += [o[j * tq:(j + 1) * tq, :] for j in range(group)]
    o_ref[...] = jnp.concatenate(outs, axis=-1).astype(o_ref.dtype)


def _gqa(qd, kd, vd, tq, tk):
    nb, _, nq, group, _, _ = qd.shape
    s = nq * tq
    qd = qd.reshape(nb, N_KV_HEADS, nq, group * tq, HEAD_DIM)
    return pl.pallas_call(
        functools.partial(_gqa_kernel, tq=tq, tk=tk, s=s),
        grid=(nb, nq),
        in_specs=[pl.BlockSpec((None, N_KV_HEADS, None, group * tq, HEAD_DIM),
                               lambda b, i: (b, 0, i, 0, 0)),
                  pl.BlockSpec((None, N_KV_HEADS, HEAD_DIM, s), lambda b, i: (b, 0, 0, 0)),
                  pl.BlockSpec((None, N_KV_HEADS, s, LANES), lambda b, i: (b, 0, 0, 0))],
        out_specs=pl.BlockSpec((None, tq, W_MIX), lambda b, i: (b, i, 0)),
        out_shape=jax.ShapeDtypeStruct((nb, s, W_MIX), BF16),
        compiler_params=_cparams(("parallel", "arbitrary")),
        name="gqa",
    )(qd, kd, vd)


def _outproj_kernel(*refs, starts):
    ns = len(starts)
    ua_ref, ub_ref, uc_ref, ud_ref = refs[:4]
    w_ref, g_ref, wr_ref, br_ref, x1_ref, xe_ref = refs[4 + ns:]
    acc = _pick_source(refs[4:4 + ns], starts, pl.program_id(0))
    for j, ref in enumerate((ua_ref, ub_ref, uc_ref, ud_ref)):
        acc = acc + jnp.dot(ref[...], w_ref[j * W_MIX:(j + 1) * W_MIX, :],
                            preferred_element_type=F32)
    x1_ref[...] = acc
    ms = jnp.mean(acc * acc, axis=-1, keepdims=True)
    xn = acc * lax.rsqrt(ms + EPS) * g_ref[...]
    logits = jnp.dot(xn.astype(BF16), wr_ref[...], preferred_element_type=F32) + br_ref[...]
    xe_ref[:, :D_MODEL] = xn
    xe_ref[:, D_MODEL:] = _route(logits)


def _route(logits):
    lane = lax.broadcasted_iota(jnp.int32, logits.shape, 1)
    big = jnp.int32(LANES)
    ninf = -jnp.inf

    def argmax(vals):
        mx = jnp.max(vals, axis=-1, keepdims=True)
        idx = jnp.min(jnp.where(vals == mx, lane, big), axis=-1, keepdims=True)
        return mx, idx

    glog = jnp.where(lane < N_GROUPS, logits, ninf)
    gmax, gidx = argmax(glog)
    g_w = 1.0 / jnp.sum(jnp.exp(glog - gmax), axis=-1, keepdims=True)
    lo = N_GROUPS + gidx * EXPERTS_PER_GROUP
    el = jnp.where((lane >= lo) & (lane < lo + EXPERTS_PER_GROUP), logits, ninf)
    t1, i1 = argmax(el)
    el2 = jnp.where(lane == i1, ninf, el)
    t2, i2 = argmax(el2)
    e2 = jnp.exp(t2 - t1)
    w1 = 1.0 / (1.0 + e2)
    w2 = e2 * w1
    gates = jnp.where(lane == i1, w1 * g_w, jnp.where(lane == i2, w2 * g_w, 0.0))
    return jnp.where(lane == 0, gidx.astype(F32), gates)


def _outproj(ua, ub, uc, ud, xsrcs, w_bf16, g2, wr, br, tm):
    xspecs, starts, ntiles = _row_source_specs(xsrcs, tm)
    n = ntiles * tm
    row = lambda i: (i, 0)
    const = lambda i: (0, 0)
    return pl.pallas_call(
        functools.partial(_outproj_kernel, starts=starts),
        grid=(ntiles,),
        in_specs=[pl.BlockSpec((tm, W_MIX), row)] * 4 + xspecs
        + [pl.BlockSpec((D_MODEL, D_MODEL), const),
           pl.BlockSpec((1, D_MODEL), const),
           pl.BlockSpec((D_MODEL, LANES), const),
           pl.BlockSpec((1, LANES), const)],
        out_specs=[pl.BlockSpec((tm, D_MODEL), row),
                   pl.BlockSpec((tm, D_ROW), row)],
        out_shape=[jax.ShapeDtypeStruct((n, D_MODEL), F32),
                   jax.ShapeDtypeStruct((n, D_ROW), F32)],
        compiler_params=_cparams(("parallel",)),
        name="outproj_router",
    )(ua, ub, uc, ud, *xsrcs, w_bf16, g2.reshape(1, D_MODEL), wr, br)


def _moe_plan(gidx, tm):
    n = gidx.shape[0]
    nt = n // tm + N_GROUPS
    groups = jnp.arange(N_GROUPS, dtype=jnp.int32)
    counts = jnp.sum((gidx[:, None] == groups[None, :]).astype(jnp.int32), axis=0)
    _, order = lax.sort((gidx, jnp.arange(n, dtype=jnp.int32)), num_keys=1, is_stable=True)
    cstart = jnp.cumsum(counts) - counts
    gtiles = (counts + tm - 1) // tm
    tstart = jnp.cumsum(gtiles) - gtiles
    t = jnp.arange(nt, dtype=jnp.int32)
    tgroup = jnp.clip(jnp.sum((t[:, None] >= tstart[None, :]).astype(jnp.int32), axis=1) - 1,
                      0, N_GROUPS - 1)
    p = jnp.arange(nt * tm, dtype=jnp.int32)
    room = N_GROUPS * tm
    ext = jnp.concatenate([jnp.zeros((room,), jnp.int32), order, jnp.zeros((room,), jnp.int32)])
    src = jnp.full((nt * tm,), -1, jnp.int32)
    for g in range(N_GROUPS):
        lo = tstart[g] * tm
        shifted = lax.dynamic_slice(ext, (room - (lo - cstart[g]),), (nt * tm,))
        src = jnp.where((p >= lo) & (p < lo + counts[g]), shifted, src)
    trash = n + p % (2 * tm)
    src_g = jnp.where(src < 0, 0, src).reshape(nt, 1, tm)
    src_s = jnp.where(src < 0, trash, src).reshape(nt, 1, tm)
    return tgroup, src_g, src_s


def _moe_kernel(tg_ref, srcg_cur_ref, srcg_nxt_ref, srcs_ref, xe_hbm, wg_ref, wu_ref, wd_ref,
                y_hbm, xg, ys, gsem, ssem, *, tm, n):
    t = pl.program_id(0)
    nt = pl.num_programs(0)
    slot = t % 2

    def row_gather(src_ref, dst_slot):
        def body(r, c):
            tok = src_ref[0, r]
            pltpu.make_async_copy(xe_hbm.at[tok], xg.at[dst_slot, r], gsem.at[dst_slot]).start()
            return c
        lax.fori_loop(0, tm, body, 0, unroll=True)

    def tile_gather_wait(s_):
        pltpu.make_async_copy(xe_hbm.at[pl.ds(0, tm)], xg.at[s_], gsem.at[s_]).wait()

    def tile_scatter_wait(s_):
        pltpu.make_async_copy(ys.at[s_], y_hbm.at[pl.ds(0, tm)], ssem.at[s_]).wait()

    @pl.when(t == 0)
    def _():
        row_gather(srcg_cur_ref, 0)
        ys[1] = jnp.zeros((tm, D_MODEL), F32)
        for half in range(2):
            cp = pltpu.make_async_copy(ys.at[1], y_hbm.at[pl.ds(n + half * tm, tm)], ssem.at[1])
            cp.start()
            cp.wait()

    tile_gather_wait(slot)

    @pl.when(t >= 2)
    def _():
        tile_scatter_wait(slot)

    row_gather(srcg_nxt_ref, 1 - slot)

    g = tg_ref[t]
    xe = xg[slot]
    x = xe[:, :D_MODEL].astype(BF16)
    gates = xe[:, D_MODEL:]
    lane = lax.broadcasted_iota(jnp.int32, gates.shape, 1)
    first = N_GROUPS + g * EXPERTS_PER_GROUP
    y = jnp.zeros((tm, D_MODEL), F32)
    for j in range(EXPERTS_PER_GROUP):
        gj = jnp.sum(jnp.where(lane == first + j, gates, 0.0), axis=-1, keepdims=True)
        a = jnp.dot(x, wg_ref[j], preferred_element_type=F32)
        u = jnp.dot(x, wu_ref[j], preferred_element_type=F32)
        hdn = (a * jax.nn.sigmoid(a) * u).astype(BF16)
        y = y + gj * jnp.dot(hdn, wd_ref[j], preferred_element_type=F32)
    ys[slot] = y

    def scat(r, c):
        tok = srcs_ref[0, r]
        pltpu.make_async_copy(ys.at[slot, r], y_hbm.at[tok], ssem.at[slot]).start()
        return c
    lax.fori_loop(0, tm, scat, 0, unroll=True)

    @pl.when(t == nt - 1)
    def _():
        tile_gather_wait(1 - slot)
        tile_scatter_wait(slot)

        @pl.when(nt >= 2)
        def _():
            tile_scatter_wait(1 - slot)


def _moe(xe, wg, wu, wd, tm):
    n = xe.shape[0]
    gidx = xe[:, D_MODEL].astype(jnp.int32)
    tgroup, src_g, src_s = _moe_plan(gidx, tm)
    nt = tgroup.shape[0]
    grp = lambda a: a.reshape((N_GROUPS, EXPERTS_PER_GROUP) + a.shape[1:])
    smem = lambda imap: pl.BlockSpec((None, 1, tm), imap, memory_space=pltpu.SMEM)
    wspec = lambda shp: pl.BlockSpec((None, EXPERTS_PER_GROUP) + shp, lambda t, tg: (tg[t], 0, 0, 0))
    grid_spec = pltpu.PrefetchScalarGridSpec(
        num_scalar_prefetch=1,
        grid=(nt,),
        in_specs=[smem(lambda t, tg: (t, 0, 0)),
                  smem(lambda t, tg: (jnp.minimum(t + 1, nt - 1), 0, 0)),
                  smem(lambda t, tg: (t, 0, 0)),
                  pl.BlockSpec(memory_space=pl.ANY),
                  wspec((D_MODEL, D_EXPERT)),
                  wspec((D_MODEL, D_EXPERT)),
                  wspec((D_EXPERT, D_MODEL))],
        out_specs=pl.BlockSpec(memory_space=pl.ANY),
        scratch_shapes=[pltpu.VMEM((2, tm, D_ROW), F32),
                        pltpu.VMEM((2, tm, D_MODEL), F32),
                        pltpu.SemaphoreType.DMA((2,)),
                        pltpu.SemaphoreType.DMA((2,))])
    return pl.pallas_call(
        functools.partial(_moe_kernel, tm=tm, n=n),
        grid_spec=grid_spec,
        out_shape=jax.ShapeDtypeStruct((n + 2 * tm, D_MODEL), F32),
        compiler_params=_cparams(("arbitrary",)),
        name="moe",
    )(tgroup, src_g, src_g, src_s, xe, grp(wg), grp(wu), grp(wd))


def _residual_kernel(x1_ref, y_ref, *o_refs, starts):
    i = pl.program_id(0)
    ends = starts[1:] + (pl.num_programs(0),)
    for o_ref, end in zip(o_refs, ends):
        @pl.when(i < end)
        def _():
            o_ref[...] = x1_ref[...] + y_ref[...]


def _residual(x1, y, tm, row_splits):
    row = lambda i: (i, 0)
    outs = [jax.ShapeDtypeStruct((r, D_MODEL), F32) for r in row_splits]
    ospecs, starts, ntiles = _row_source_specs(outs, tm)
    return pl.pallas_call(
        functools.partial(_residual_kernel, starts=starts),
        grid=(ntiles,),
        in_specs=[pl.BlockSpec((tm, D_MODEL), row), pl.BlockSpec((tm, D_MODEL), row)],
        out_specs=ospecs,
        out_shape=outs,
        compiler_params=_cparams(("arbitrary",)),
        name="moe_residual",
    )(x1, y)


def _tiles(nb, s):
    return dict(tm=min(512, s), tconv=min(512, s),
                tq=min(512, s), tk=min(2048, s), tmoe=min(512, s))


def _layer(xsrcs, nb, s, p, l, tables, tl, row_splits):
    n = nb * s
    pa, pb, pc, pd = _inproj(xsrcs, p["norm1_g"][l], p["w_in"][l], tl["tm"])
    pa = pa.reshape(nb, s, SZ_A)
    pb = pb.reshape(nb, s, SZ_B)
    pc = pc.reshape(nb, s, SZ_C)
    pd = pd.reshape(nb, s, SZ_D)
    ua, ub = _conv_mixers(pa, pb, p["conv_a_w"][l], p["conv_a_b"][l], p["ln_a_g"][l],
                          p["ln_a_b"][l], p["conv_b_w"][l], tl["tconv"])
    gains = jnp.stack([jnp.tile(p[k][l], N_HEADS) for k in ("qn_c", "kn_c", "qn_d", "kn_d")])
    c1, c4, c16, qd, kd, vd = _prep(pc, pd, gains, tables, tl["tq"])
    uc = _dilated_mixture((c1, c4, c16), tl["tm"])
    ud = _gqa(qd, kd, vd, tl["tq"], tl["tk"])
    x1, xe = _outproj(ua.reshape(n, W_MIX), ub.reshape(n, W_MIX), uc.reshape(n, W_MIX),
                      ud.reshape(n, W_MIX), xsrcs, p["w_out"][l], p["norm2_g"][l],
                      p["w_route"][l], p["b_route"][l], tl["tm"])
    y = _moe(xe, p["w_gate"][l], p["w_up"][l], p["w_down"][l], tl["tmoe"])
    return _residual(x1, y, tl["tm"], row_splits)


def _trunk(xs, p):
    s = xs[0].shape[1]
    assert all(x.shape[1:] == (s, D_MODEL) for x in xs)
    nb = sum(x.shape[0] for x in xs)
    tables = _rope_tables(s)
    tl = _tiles(nb, s)
    depth = p["w_in"].shape[0]
    srcs = [x.reshape(-1, D_MODEL) for x in xs]
    for l in range(depth):
        splits = [x.shape[0] * s for x in xs] if l == depth - 1 else [nb * s]
        srcs = _layer(srcs, nb, s, p, l, tables, tl, splits)
    return tuple(y.reshape(x.shape) for y, x in zip(srcs, xs))


def _pack_params(norm1_g, w_in, conv_a_w, conv_a_b, ln_a_g, ln_a_b, conv_b_w, qn_c, kn_c, qn_d, kn_d,
                 w_out, norm2_g, w_group, b_group, w_router, b_router, w_gate, w_up, w_down):
    depth = w_in.shape[0]
    pad = LANES - N_GROUPS - N_EXPERTS
    w_route = jnp.concatenate([w_group, w_router, jnp.zeros((depth, D_MODEL, pad), F32)], axis=-1)
    b_route = jnp.concatenate([b_group, b_router, jnp.zeros((depth, pad), F32)], axis=-1)
    return dict(norm1_g=norm1_g, w_in=w_in.astype(BF16), conv_a_w=conv_a_w, conv_a_b=conv_a_b,
                ln_a_g=ln_a_g, ln_a_b=ln_a_b, conv_b_w=conv_b_w, qn_c=qn_c, kn_c=kn_c, qn_d=qn_d,
                kn_d=kn_d, w_out=w_out.astype(BF16), norm2_g=norm2_g,
                w_route=w_route.astype(BF16), b_route=b_route.reshape(depth, 1, LANES),
                w_gate=w_gate.astype(BF16), w_up=w_up.astype(BF16), w_down=w_down.astype(BF16))


def kernel(x_prompt, x_sample, norm1_g, w_in, conv_a_w, conv_a_b, ln_a_g, ln_a_b, conv_b_w, qn_c, kn_c, qn_d, kn_d, w_out, norm2_g, w_group, b_group, w_router, b_router, w_gate, w_up, w_down):
    p = _pack_params(norm1_g, w_in, conv_a_w, conv_a_b, ln_a_g, ln_a_b, conv_b_w, qn_c, kn_c, qn_d,
                     kn_d, w_out, norm2_g, w_group, b_group, w_router, b_router, w_gate, w_up, w_down)
    return _trunk((x_prompt, x_sample), p)
```

```python
import functools
import math

import jax
import jax.numpy as jnp
from jax import lax
from jax.experimental import pallas as pl
from jax.experimental.pallas import tpu as pltpu

D_MODEL = 1024
HEAD_DIM = 64
W_MIX = 256
N_HEADS = 4
N_KV_HEADS = 2
CONV_A_WIDTH = 31
CONV_B_WIDTH = 3
DIL_PATTERNS = ((128, 1), (512, 4), (2048, 16))
Q_BLOCK = 128
GRID_W = 64
ROPE_THETA = 10000.0
N_GROUPS = 4
EXPERTS_PER_GROUP = 4
N_EXPERTS = 16
PAIRS_PER_GROUP = EXPERTS_PER_GROUP * (EXPERTS_PER_GROUP - 1) // 2
N_CLASSES = N_GROUPS * PAIRS_PER_GROUP
D_EXPERT = 512
EPS = 1e-6
NEG_INF = -1e30
LOG2E = math.log2(math.e)

SZ_A, SZ_B, SZ_C, SZ_D = 512, 768, 768, 512
D_IN = SZ_A + SZ_B + SZ_C + SZ_D
LANES = 128
SUBLANES = 8
D_ROW = D_MODEL + LANES
HALO = 16
VMEM_LIMIT = 56 * 1024 * 1024

F32 = jnp.float32
BF16 = jnp.bfloat16


def _cparams(sem):
    return pltpu.CompilerParams(dimension_semantics=sem, vmem_limit_bytes=VMEM_LIMIT)


def _tile_of_source(i, *_, start, count):
    return (jnp.clip(i - start, 0, count - 1), 0)


def _row_source_specs(srcs, tm):
    specs, starts, start = [], [], 0
    for a in srcs:
        count = a.shape[0] // tm
        specs.append(pl.BlockSpec((tm, a.shape[1]),
                                  functools.partial(_tile_of_source, start=start, count=count)))
        starts.append(start)
        start += count
    return specs, tuple(starts), start


def _pick_source(refs, starts, i):
    x = refs[0][...]
    for ref, start in zip(refs[1:], starts[1:]):
        x = jnp.where(i >= start, ref[...], x)
    return x


def _inproj_kernel(*refs, starts):
    ns = len(starts)
    g_ref, w_ref, pa_ref, pb_ref, pc_ref, pd_ref = refs[ns:]
    x = _pick_source(refs[:ns], starts, pl.program_id(0))
    ms = jnp.mean(x * x, axis=-1, keepdims=True)
    h = (x * lax.rsqrt(ms + EPS) * g_ref[...]).astype(BF16)
    off = 0
    for ref, sz in ((pa_ref, SZ_A), (pb_ref, SZ_B), (pc_ref, SZ_C), (pd_ref, SZ_D)):
        ref[...] = jnp.dot(h, w_ref[:, off:off + sz], preferred_element_type=F32)
        off += sz


def _inproj(xsrcs, g, w_bf16, tm):
    xspecs, starts, ntiles = _row_source_specs(xsrcs, tm)
    n = ntiles * tm
    outs = [jax.ShapeDtypeStruct((n, sz), F32) for sz in (SZ_A, SZ_B, SZ_C, SZ_D)]
    return pl.pallas_call(
        functools.partial(_inproj_kernel, starts=starts),
        grid=(ntiles,),
        in_specs=xspecs + [pl.BlockSpec((1, D_MODEL), lambda i: (0, 0)),
                           pl.BlockSpec((D_MODEL, D_IN), lambda i: (0, 0))],
        out_specs=[pl.BlockSpec((tm, sz), lambda i: (i, 0)) for sz in (SZ_A, SZ_B, SZ_C, SZ_D)],
        out_shape=outs,
        compiler_params=_cparams(("parallel",)),
        name="inproj",
    )(*xsrcs, g.reshape(1, D_MODEL), w_bf16)


def _conv_kernel(pa_ref, pa_prev_ref, pa_next_ref, pb_ref, pb_prev_ref, pb_next_ref,
                 wa_ref, ba_ref, lg_ref, lb_ref, wb_ref, ua_ref, ub_ref, bufa, bufb, *, t):
    si = pl.program_id(1)
    ns = pl.num_programs(1)
    has_prev = (si > 0).astype(F32)
    has_next = (si < ns - 1).astype(F32)

    def glu(ref):
        p = ref[...]
        return p[:, :W_MIX] * jax.nn.sigmoid(p[:, W_MIX:])

    def gated(ref):
        p = ref[...]
        return p[:, W_MIX:2 * W_MIX] * p[:, 2 * W_MIX:]

    bufa[0:HALO, :] = glu(pa_prev_ref) * has_prev
    bufa[HALO:HALO + t, :] = glu(pa_ref)
    bufa[HALO + t:2 * HALO + t, :] = glu(pa_next_ref) * has_next
    bufb[0:HALO, :] = gated(pb_prev_ref) * has_prev
    bufb[HALO:HALO + t, :] = gated(pb_ref)
    bufb[HALO + t:2 * HALO + t, :] = gated(pb_next_ref) * has_next

    first = HALO - CONV_A_WIDTH // 2
    acc = jnp.zeros((t, W_MIX), F32)
    for b in range(SUBLANES):
        z = None
        for a in range((2 * HALO) // SUBLANES):
            k = SUBLANES * a + b - first
            if 0 <= k < CONV_A_WIDTH:
                term = bufa[SUBLANES * a:SUBLANES * a + t + SUBLANES, :] * wa_ref[k:k + 1, :]
                z = term if z is None else z + term
        acc = acc + z[b:b + t, :]
    acc = acc + ba_ref[...]
    mu = jnp.mean(acc, axis=-1, keepdims=True)
    xc = acc - mu
    var = jnp.mean(xc * xc, axis=-1, keepdims=True)
    y = xc * lax.rsqrt(var + EPS) * lg_ref[...] + lb_ref[...]
    ua_ref[...] = (y * jax.nn.sigmoid(y)).astype(BF16)

    accb = jnp.zeros((t, W_MIX), F32)
    for k in range(CONV_B_WIDTH):
        start = HALO - CONV_B_WIDTH // 2 + k
        accb = accb + bufb[start:start + t, :] * wb_ref[k:k + 1, :]
    ub_ref[...] = (pb_ref[:, :W_MIX] * accb).astype(BF16)


def _conv_mixers(pa, pb, wa, ba, lg, lb, wb, t):
    nb, s, _ = pa.shape
    hb = t // HALO
    last = s // HALO - 1

    def main(b, i):
        return (b, i, 0)

    def prev(b, i):
        return (b, jnp.maximum(i * hb - 1, 0), 0)

    def nxt(b, i):
        return (b, jnp.minimum((i + 1) * hb, last), 0)

    const = lambda b, i: (0, 0)
    return pl.pallas_call(
        functools.partial(_conv_kernel, t=t),
        grid=(nb, s // t),
        in_specs=[pl.BlockSpec((None, t, SZ_A), main),
                  pl.BlockSpec((None, HALO, SZ_A), prev),
                  pl.BlockSpec((None, HALO, SZ_A), nxt),
                  pl.BlockSpec((None, t, SZ_B), main),
                  pl.BlockSpec((None, HALO, SZ_B), prev),
                  pl.BlockSpec((None, HALO, SZ_B), nxt),
                  pl.BlockSpec((CONV_A_WIDTH, W_MIX), const),
                  pl.BlockSpec((1, W_MIX), const),
                  pl.BlockSpec((1, W_MIX), const),
                  pl.BlockSpec((1, W_MIX), const),
                  pl.BlockSpec((CONV_B_WIDTH, W_MIX), const)],
        out_specs=[pl.BlockSpec((None, t, W_MIX), main),
                   pl.BlockSpec((None, t, W_MIX), main)],
        out_shape=[jax.ShapeDtypeStruct((nb, s, W_MIX), BF16)] * 2,
        scratch_shapes=[pltpu.VMEM((t + 2 * HALO, W_MIX), F32),
                        pltpu.VMEM((t + 2 * HALO, W_MIX), F32)],
        compiler_params=_cparams(("parallel", "parallel")),
        name="conv_mixers",
    )(pa, pa, pa, pb, pb, pb, wa, ba.reshape(1, W_MIX), lg.reshape(1, W_MIX),
      lb.reshape(1, W_MIX), wb)


def _head_sumsq(x, ones_bd):
    x2 = x * x
    hi = x2.astype(BF16)
    lo = (x2 - hi.astype(F32)).astype(BF16)
    return (jnp.dot(hi, ones_bd, preferred_element_type=F32)
            + jnp.dot(lo, ones_bd, preferred_element_type=F32))


def _rope(x, cos, sin_signed, half):
    w = x.shape[-1]
    lane = lax.broadcasted_iota(jnp.int32, x.shape, 1)
    fwd = pltpu.roll(x, w - half, 1)
    bwd = pltpu.roll(x, half, 1)
    rot = jnp.where(lane % (2 * half) < half, fwd, bwd)
    return x * cos + rot * sin_signed


def _prep_kernel(pc_ref, pd_ref, gains_ref, cosc_ref, sinc_ref, cosd_ref, sind_ref, ones_ref,
                 c1_ref, c4_ref, c16_ref, qd_ref, kd_ref, vd_ref, qkv_scr):
    c_refs = (c1_ref, c4_ref, c16_ref)
    scale = HEAD_DIM ** -0.5
    ones_bd = ones_ref[...]
    pc = pc_ref[...]
    pd = pd_ref[...]
    gains = gains_ref[...]
    cosc, sinc = cosc_ref[...], sinc_ref[...]
    cosd, sind = cosd_ref[...], sind_ref[...]

    def normed(x, g, ones):
        ss = _head_sumsq(x, ones)
        return x * lax.rsqrt(ss * (1.0 / HEAD_DIM) + EPS) * g

    qc = _rope(normed(pc[:, 0:256], gains[0:1, :], ones_bd), cosc, sinc, HEAD_DIM // 2) * scale
    kc = _rope(normed(pc[:, 256:512], gains[1:2, :], ones_bd), cosc, sinc, HEAD_DIM // 2)
    qkv = (qc, kc, pc[:, 512:768])
    ncb = SZ_C // LANES
    for c in range(ncb):
        part = qkv[c * LANES // W_MIX]
        lo = c * LANES % W_MIX
        qkv_scr[c] = part[:, lo:lo + LANES]
    ts = pc.shape[0]
    for ref, (_, dil) in zip(c_refs, DIL_PATTERNS):
        for r in range(dil):
            for c in range(ncb):
                ref[r, :, c * LANES:(c + 1) * LANES] = (
                    qkv_scr[c, pl.ds(r, ts // dil, stride=dil), :].astype(BF16))

    qd = _rope(normed(pd[:, 0:256], gains[2:3, :], ones_bd), cosd, sind, HEAD_DIM // 4) * (scale * LOG2E)
    kd = _rope(normed(pd[:, 256:384], gains[3:4, 0:128], ones_bd[0:128, 0:128]),
               cosd[:, 0:128], sind[:, 0:128], HEAD_DIM // 4)
    vd = pd[:, 384:512]
    group = N_HEADS // N_KV_HEADS
    for h in range(N_HEADS):
        qd_ref[h // group, h % group] = qd[:, h * HEAD_DIM:(h + 1) * HEAD_DIM].astype(BF16)
    lane = lax.broadcasted_iota(jnp.int32, vd.shape, 1)
    kd_t = kd.T.astype(BF16)
    for h in range(N_KV_HEADS):
        kd_ref[h] = kd_t[h * HEAD_DIM:(h + 1) * HEAD_DIM, :]
        vh = vd if h == 0 else pltpu.roll(vd, LANES - h * HEAD_DIM, 1)
        vd_ref[h] = jnp.where(lane < HEAD_DIM, vh, jnp.where(lane == HEAD_DIM, 1.0, 0.0)).astype(BF16)


def _rope_tables(s):
    def angles(pos, dim):
        inv = ROPE_THETA ** (-jnp.arange(0, dim, 2, dtype=F32) / dim)
        return pos.astype(F32)[:, None] * inv[None, :]

    pos = jnp.arange(s)
    ang = angles(pos, HEAD_DIM)
    cos_c = jnp.tile(jnp.concatenate([jnp.cos(ang), jnp.cos(ang)], -1), (1, N_HEADS))
    sin_c = jnp.tile(jnp.concatenate([-jnp.sin(ang), jnp.sin(ang)], -1), (1, N_HEADS))
    row = pos // GRID_W
    col = pos % GRID_W
    ar = angles(row, HEAD_DIM // 2)
    ac = angles(col, HEAD_DIM // 2)
    cos_d = jnp.tile(jnp.concatenate([jnp.cos(ar), jnp.cos(ar), jnp.cos(ac), jnp.cos(ac)], -1),
                     (1, N_HEADS))
    sin_d = jnp.tile(jnp.concatenate([-jnp.sin(ar), jnp.sin(ar), -jnp.sin(ac), jnp.sin(ac)], -1),
                     (1, N_HEADS))
    return cos_c, sin_c, cos_d, sin_d


def _prep(pc, pd, gains, tables, ts):
    nb, s, _ = pc.shape
    cos_c, sin_c, cos_d, sin_d = tables
    head = lax.broadcasted_iota(jnp.int32, (W_MIX, W_MIX), 0) // HEAD_DIM
    ones_bd = (head == head.T).astype(BF16)
    tok = lambda b, i: (b, i, 0)
    tab = lambda b, i: (i, 0)
    const = lambda b, i: (0, 0)
    hm = lambda b, i: (b, 0, i, 0)
    group = N_HEADS // N_KV_HEADS
    return pl.pallas_call(
        _prep_kernel,
        grid=(nb, s // ts),
        in_specs=[pl.BlockSpec((None, ts, SZ_C), tok),
                  pl.BlockSpec((None, ts, SZ_D), tok),
                  pl.BlockSpec((4, W_MIX), const),
                  pl.BlockSpec((ts, W_MIX), tab),
                  pl.BlockSpec((ts, W_MIX), tab),
                  pl.BlockSpec((ts, W_MIX), tab),
                  pl.BlockSpec((ts, W_MIX), tab),
                  pl.BlockSpec((W_MIX, W_MIX), const)],
        out_specs=[pl.BlockSpec((None, dil, ts // dil, SZ_C), hm) for _, dil in DIL_PATTERNS]
        + [pl.BlockSpec((None, N_KV_HEADS, None, group, ts, HEAD_DIM),
                                lambda b, i: (b, 0, i, 0, 0, 0)),
                   pl.BlockSpec((None, N_KV_HEADS, HEAD_DIM, ts), lambda b, i: (b, 0, 0, i)),
                   pl.BlockSpec((None, N_KV_HEADS, ts, LANES), hm)],
        out_shape=[jax.ShapeDtypeStruct((nb, dil, s // dil, SZ_C), BF16) for _, dil in DIL_PATTERNS]
        + [jax.ShapeDtypeStruct((nb, N_KV_HEADS, s // ts, group, ts, HEAD_DIM), BF16),
           jax.ShapeDtypeStruct((nb, N_KV_HEADS, HEAD_DIM, s), BF16),
           jax.ShapeDtypeStruct((nb, N_KV_HEADS, s, LANES), BF16)],
        scratch_shapes=[pltpu.VMEM((SZ_C // LANES, ts, LANES), F32)],
        compiler_params=_cparams(("parallel", "parallel")),
        name="qk_prep",
    )(pc, pd, gains, cos_c, sin_c, cos_d, sin_d, ones_bd)


def _dilated_kernel(q_ref, k_ref, v_ref, o_ref, lse_ref, *, seq_sub, half, lq):
    span = Q_BLOCK + 2 * half
    qi = pl.program_id(2)
    nt = (((1,), (1,)), ((), ()))
    for i in range(lq // Q_BLOCK):
        q0 = i * Q_BLOCK
        qabs = qi * lq + q0
        ks = pl.multiple_of(jnp.clip(qabs - half, 0, seq_sub - span), half)
        q = q_ref[q0:q0 + Q_BLOCK, :]
        k = k_ref[pl.ds(ks, span), :]
        v = v_ref[pl.ds(ks, span), :]
        qpos = qabs + lax.broadcasted_iota(jnp.int32, (Q_BLOCK, span), 0)
        kpos = ks + lax.broadcasted_iota(jnp.int32, (Q_BLOCK, span), 1)
        mask = jnp.abs(kpos - qpos) <= half
        o_parts, l_parts = [], []
        for h in range(N_HEADS):
            hs = slice(h * HEAD_DIM, (h + 1) * HEAD_DIM)
            sc = lax.dot_general(q[:, hs], k[:, hs], nt, preferred_element_type=F32)
            sc = jnp.where(mask, sc, NEG_INF)
            m = jnp.max(sc, axis=-1, keepdims=True)
            p = jnp.exp(sc - m)
            l = jnp.sum(p, axis=-1, keepdims=True)
            num = jnp.dot(p.astype(BF16), v[:, hs], preferred_element_type=F32)
            o_parts.append(num / l)
            l_parts.append(jnp.broadcast_to(m + jnp.log(l), (Q_BLOCK, HEAD_DIM)))
        o_ref[q0:q0 + Q_BLOCK, :] = jnp.concatenate(o_parts, axis=-1)
        lse_ref[q0:q0 + Q_BLOCK, :] = jnp.concatenate(l_parts, axis=-1)


def _dilated_pattern(qkv, win, dil):
    nb, _, seq_sub, _ = qkv.shape
    half = win // (2 * dil)
    lq = min(seq_sub, 512)
    qspec = pl.BlockSpec((None, None, lq, W_MIX), lambda b, r, i: (b, r, i, 0))
    kspec = pl.BlockSpec((None, None, seq_sub, W_MIX), lambda b, r, i: (b, r, 0, 1))
    vspec = pl.BlockSpec((None, None, seq_sub, W_MIX), lambda b, r, i: (b, r, 0, 2))
    return pl.pallas_call(
        functools.partial(_dilated_kernel, seq_sub=seq_sub, half=half, lq=lq),
        grid=(nb, dil, seq_sub // lq),
        in_specs=[qspec, kspec, vspec],
        out_specs=[qspec, qspec],
        out_shape=[jax.ShapeDtypeStruct((nb, dil, seq_sub, W_MIX), F32)] * 2,
        compiler_params=_cparams(("parallel", "parallel", "arbitrary")),
        name=f"dilated_w{win}_d{dil}",
    )(qkv, qkv, qkv)


def _mixture_kernel(*refs, ts):
    n = len(DIL_PATTERNS)
    o_refs, l_refs = refs[0:2 * n:2], refs[1:2 * n:2]
    uc_ref = refs[2 * n]
    scr = refs[2 * n + 1:]

    def natural(ref, dil, buf):
        if dil == 1:
            return ref[0]
        ncb = W_MIX // LANES
        for r in range(dil):
            for c in range(ncb):
                buf[c, pl.ds(r, ts // dil, stride=dil), :] = ref[r, :, c * LANES:(c + 1) * LANES]
        return jnp.concatenate([buf[c] for c in range(ncb)], axis=-1)

    os_, ls_ = [], []
    for j, (_, dil) in enumerate(DIL_PATTERNS):
        os_.append(natural(o_refs[j], dil, scr[2 * j]))
        ls_.append(natural(l_refs[j], dil, scr[2 * j + 1]))
    mx = functools.reduce(jnp.maximum, ls_)
    ws = [jnp.exp(l - mx) for l in ls_]
    num = sum(w * o for w, o in zip(ws, os_))
    uc_ref[...] = (num / sum(ws)).astype(uc_ref.dtype)


def _dilated_mixture(c_layouts, ts):
    nb, _, s, _ = c_layouts[0].shape
    ins, in_specs = [], []
    for qkv, (win, dil) in zip(c_layouts, DIL_PATTERNS):
        o, lse = _dilated_pattern(qkv, win, dil)
        spec = pl.BlockSpec((None, dil, ts // dil, W_MIX), lambda b, i: (b, 0, i, 0))
        ins += [o, lse]
        in_specs += [spec, spec]
    return pl.pallas_call(
        functools.partial(_mixture_kernel, ts=ts),
        grid=(nb, s // ts),
        in_specs=in_specs,
        out_specs=pl.BlockSpec((None, ts, W_MIX), lambda b, i: (b, i, 0)),
        out_shape=jax.ShapeDtypeStruct((nb, s, W_MIX), BF16),
        scratch_shapes=[pltpu.VMEM((W_MIX // LANES, ts, LANES), F32)] * (2 * len(DIL_PATTERNS)),
        compiler_params=_cparams(("parallel", "parallel")),
        name="dilated_mixture",
    )(*ins)


def _gqa_kernel(q_ref, k_ref, v_ref, o_ref, *, tq, tk, s):
    group = N_HEADS // N_KV_HEADS
    rows = group * tq
    qs = [q_ref[g] for g in range(N_KV_HEADS)]

    def body(c, carry):
        k0 = pl.multiple_of(c * tk, tk)
        new = []
        for g in range(N_KV_HEADS):
            kt = k_ref[g, :, pl.ds(k0, tk)]
            v = v_ref[g, pl.ds(k0, tk), :]
            m, acc = carry[g]
            sc = jnp.dot(qs[g], kt, preferred_element_type=F32)
            m_new = jnp.maximum(m, jnp.max(sc, axis=-1, keepdims=True))
            alpha = jnp.exp2(m - m_new)
            p = jnp.exp2(sc - m_new).astype(BF16)
            acc = alpha * acc + jnp.dot(p, v, preferred_element_type=F32)
            new.append((m_new, acc))
        return tuple(new)

    init = tuple((jnp.full((rows, 1), -jnp.inf, F32), jnp.zeros((rows, LANES), F32))
                 for _ in range(N_KV_HEADS))
    fin = lax.fori_loop(0, s // tk, body, init)
    outs = []
    for (_, acc) in fin:
        o = acc[:, :HEAD_DIM] / acc[:, HEAD_DIM:HEAD_DIM + 1]
        outs += [o[j * tq:(j + 1) * tq, :] for j in range(group)]
    o_ref[...] = jnp.concatenate(outs, axis=-1).astype(o_ref.dtype)


def _gqa(qd, kd, vd, tq, tk):
    nb, _, nq, group, _, _ = qd.shape
    s = nq * tq
    qd = qd.reshape(nb, N_KV_HEADS, nq, group * tq, HEAD_DIM)
    return pl.pallas_call(
        functools.partial(_gqa_kernel, tq=tq, tk=tk, s=s),
        grid=(nb, nq),
        in_specs=[pl.BlockSpec((None, N_KV_HEADS, None, group * tq, HEAD_DIM),
                               lambda b, i: (b, 0, i, 0, 0)),
                  pl.BlockSpec((None, N_KV_HEADS, HEAD_DIM, s), lambda b, i: (b, 0, 0, 0)),
                  pl.BlockSpec((None, N_KV_HEADS, s, LANES), lambda b, i: (b, 0, 0, 0))],
        out_specs=pl.BlockSpec((None, tq, W_MIX), lambda b, i: (b, i, 0)),
        out_shape=jax.ShapeDtypeStruct((nb, s, W_MIX), BF16),
        compiler_params=_cparams(("parallel", "arbitrary")),
        name="gqa",
    )(qd, kd, vd)


def _outproj_kernel(*refs, starts):
    ns = len(starts)
    ua_ref, ub_ref, uc_ref, ud_ref = refs[:4]
    w_ref, g_ref, wr_ref, br_ref, x1_ref, xe_ref = refs[4 + ns:]
    acc = _pick_source(refs[4:4 + ns], starts, pl.program_id(0))
    for j, ref in enumerate((ua_ref, ub_ref, uc_ref, ud_ref)):
        acc = acc + jnp.dot(ref[...], w_ref[j * W_MIX:(j + 1) * W_MIX, :],
                            preferred_element_type=F32)
    x1_ref[...] = acc
    ms = jnp.mean(acc * acc, axis=-1, keepdims=True)
    xn = acc * lax.rsqrt(ms + EPS) * g_ref[...]
    logits = jnp.dot(xn.astype(BF16), wr_ref[...], preferred_element_type=F32) + br_ref[...]
    xe_ref[:, :D_MODEL] = xn
    xe_ref[:, D_MODEL:] = _route(logits)


def _route(logits):
    lane = lax.broadcasted_iota(jnp.int32, logits.shape, 1)
    big = jnp.int32(LANES)
    ninf = -jnp.inf

    def argmax(vals):
        mx = jnp.max(vals, axis=-1, keepdims=True)
        idx = jnp.min(jnp.where(vals == mx, lane, big), axis=-1, keepdims=True)
        return mx, idx

    glog = jnp.where(lane < N_GROUPS, logits, ninf)
    gmax, gidx = argmax(glog)
    g_w = 1.0 / jnp.sum(jnp.exp(glog - gmax), axis=-1, keepdims=True)
    lo = N_GROUPS + gidx * EXPERTS_PER_GROUP
    el = jnp.where((lane >= lo) & (lane < lo + EXPERTS_PER_GROUP), logits, ninf)
    t1, i1 = argmax(el)
    el2 = jnp.where(lane == i1, ninf, el)
    t2, i2 = argmax(el2)
    e2 = jnp.exp(t2 - t1)
    w1 = 1.0 / (1.0 + e2)
    w2 = e2 * w1
    gates = jnp.where(lane == i1, w1 * g_w, jnp.where(lane == i2, w2 * g_w, 0.0))
    a = jnp.minimum(i1, i2) - lo
    b = jnp.maximum(i1, i2) - lo
    pair = lax.shift_right_logical(a * (2 * EXPERTS_PER_GROUP - 1 - a), 1) + (b - a - 1)
    cls = gidx * PAIRS_PER_GROUP + pair
    return jnp.where(lane == 0, cls.astype(F32), gates)


def _outproj(ua, ub, uc, ud, xsrcs, w_bf16, g2, wr, br, tm):
    xspecs, starts, ntiles = _row_source_specs(xsrcs, tm)
    n = ntiles * tm
    row = lambda i: (i, 0)
    const = lambda i: (0, 0)
    return pl.pallas_call(
        functools.partial(_outproj_kernel, starts=starts),
        grid=(ntiles,),
        in_specs=[pl.BlockSpec((tm, W_MIX), row)] * 4 + xspecs
        + [pl.BlockSpec((D_MODEL, D_MODEL), const),
           pl.BlockSpec((1, D_MODEL), const),
           pl.BlockSpec((D_MODEL, LANES), const),
           pl.BlockSpec((1, LANES), const)],
        out_specs=[pl.BlockSpec((tm, D_MODEL), row),
                   pl.BlockSpec((tm, D_ROW), row)],
        out_shape=[jax.ShapeDtypeStruct((n, D_MODEL), F32),
                   jax.ShapeDtypeStruct((n, D_ROW), F32)],
        compiler_params=_cparams(("parallel",)),
        name="outproj_router",
    )(ua, ub, uc, ud, *xsrcs, w_bf16, g2.reshape(1, D_MODEL), wr, br)


def _class_experts():
    e1, e2 = [], []
    for g in range(N_GROUPS):
        for a in range(EXPERTS_PER_GROUP):
            for b in range(a + 1, EXPERTS_PER_GROUP):
                e1.append(g * EXPERTS_PER_GROUP + a)
                e2.append(g * EXPERTS_PER_GROUP + b)
    return jnp.array(e1, jnp.int32), jnp.array(e2, jnp.int32)


def _moe_plan(cls, tm):
    n = cls.shape[0]
    nt = n // tm + N_CLASSES
    classes = jnp.arange(N_CLASSES, dtype=jnp.int32)
    counts = jnp.sum((cls[:, None] == classes[None, :]).astype(jnp.int32), axis=0)
    _, order = lax.sort((cls, jnp.arange(n, dtype=jnp.int32)), num_keys=1, is_stable=True)
    cstart = jnp.cumsum(counts) - counts
    ctiles = (counts + tm - 1) // tm
    tstart = jnp.cumsum(ctiles) - ctiles
    t = jnp.arange(nt, dtype=jnp.int32)
    tclass = jnp.clip(jnp.sum((t[:, None] >= tstart[None, :]).astype(jnp.int32), axis=1) - 1,
                      0, N_CLASSES - 1)
    p = jnp.arange(nt * tm, dtype=jnp.int32)
    room = N_CLASSES * tm
    ext = jnp.concatenate([jnp.zeros((room,), jnp.int32), order, jnp.zeros((room,), jnp.int32)])
    src = jnp.full((nt * tm,), -1, jnp.int32)
    for c in range(N_CLASSES):
        lo = tstart[c] * tm
        shifted = lax.dynamic_slice(ext, (room - (lo - cstart[c]),), (nt * tm,))
        src = jnp.where((p >= lo) & (p < lo + counts[c]), shifted, src)
    trash = n + p % (2 * tm)
    src_g = jnp.where(src < 0, 0, src).reshape(nt, 1, tm)
    src_s = jnp.where(src < 0, trash, src).reshape(nt, 1, tm)
    e1, e2 = _class_experts()
    return e1[tclass], e2[tclass], src_g, src_s


def _moe_kernel(e1_ref, e2_ref, srcg_cur_ref, srcg_nxt_ref, srcs_ref, xe_hbm,
                wg1_ref, wu1_ref, wd1_ref, wg2_ref, wu2_ref, wd2_ref,
                y_hbm, xg, ys, gsem, ssem, *, tm, n):
    t = pl.program_id(0)
    nt = pl.num_programs(0)
    slot = t % 2

    def row_gather(src_ref, dst_slot):
        def body(r, c):
            tok = src_ref[0, r]
            pltpu.make_async_copy(xe_hbm.at[tok], xg.at[dst_slot, r], gsem.at[dst_slot]).start()
            return c
        lax.fori_loop(0, tm, body, 0, unroll=True)

    def tile_gather_wait(s_):
        pltpu.make_async_copy(xe_hbm.at[pl.ds(0, tm)], xg.at[s_], gsem.at[s_]).wait()

    def tile_scatter_wait(s_):
        pltpu.make_async_copy(ys.at[s_], y_hbm.at[pl.ds(0, tm)], ssem.at[s_]).wait()

    @pl.when(t == 0)
    def _():
        row_gather(srcg_cur_ref, 0)
        ys[1] = jnp.zeros((tm, D_MODEL), F32)
        for half in range(2):
            cp = pltpu.make_async_copy(ys.at[1], y_hbm.at[pl.ds(n + half * tm, tm)], ssem.at[1])
            cp.start()
            cp.wait()

    tile_gather_wait(slot)

    @pl.when(t >= 2)
    def _():
        tile_scatter_wait(slot)

    row_gather(srcg_nxt_ref, 1 - slot)

    xe = xg[slot]
    x = xe[:, :D_MODEL].astype(BF16)
    gates = xe[:, D_MODEL:]
    lane = lax.broadcasted_iota(jnp.int32, gates.shape, 1)
    y = jnp.zeros((tm, D_MODEL), F32)
    for e_ref, wg_ref, wu_ref, wd_ref in ((e1_ref, wg1_ref, wu1_ref, wd1_ref),
                                          (e2_ref, wg2_ref, wu2_ref, wd2_ref)):
        ge = jnp.sum(jnp.where(lane == N_GROUPS + e_ref[t], gates, 0.0), axis=-1, keepdims=True)
        a = jnp.dot(x, wg_ref[...], preferred_element_type=F32)
        u = jnp.dot(x, wu_ref[...], preferred_element_type=F32)
        hdn = (a * jax.nn.sigmoid(a) * u).astype(BF16)
        y = y + ge * jnp.dot(hdn, wd_ref[...], preferred_element_type=F32)
    ys[slot] = y

    def scat(r, c):
        tok = srcs_ref[0, r]
        pltpu.make_async_copy(ys.at[slot, r], y_hbm.at[tok], ssem.at[slot]).start()
        return c
    lax.fori_loop(0, tm, scat, 0, unroll=True)

    @pl.when(t == nt - 1)
    def _():
        tile_gather_wait(1 - slot)
        tile_scatter_wait(slot)

        @pl.when(nt >= 2)
        def _():
            tile_scatter_wait(1 - slot)


def _moe(xe, wg, wu, wd, tm):
    n = xe.shape[0]
    cls = xe[:, D_MODEL].astype(jnp.int32)
    e1, e2, src_g, src_s = _moe_plan(cls, tm)
    nt = e1.shape[0]
    smem = lambda imap: pl.BlockSpec((None, 1, tm), imap, memory_space=pltpu.SMEM)
    w1 = lambda shp: pl.BlockSpec((None,) + shp, lambda t, e1, e2: (e1[t], 0, 0))
    w2 = lambda shp: pl.BlockSpec((None,) + shp, lambda t, e1, e2: (e2[t], 0, 0))
    up, down = (D_MODEL, D_EXPERT), (D_EXPERT, D_MODEL)
    grid_spec = pltpu.PrefetchScalarGridSpec(
        num_scalar_prefetch=2,
        grid=(nt,),
        in_specs=[smem(lambda t, e1, e2: (t, 0, 0)),
                  smem(lambda t, e1, e2: (jnp.minimum(t + 1, nt - 1), 0, 0)),
                  smem(lambda t, e1, e2: (t, 0, 0)),
                  pl.BlockSpec(memory_space=pl.ANY),
                  w1(up), w1(up), w1(down), w2(up), w2(up), w2(down)],
        out_specs=pl.BlockSpec(memory_space=pl.ANY),
        scratch_shapes=[pltpu.VMEM((2, tm, D_ROW), F32),
                        pltpu.VMEM((2, tm, D_MODEL), F32),
                        pltpu.SemaphoreType.DMA((2,)),
                        pltpu.SemaphoreType.DMA((2,))])
    return pl.pallas_call(
        functools.partial(_moe_kernel, tm=tm, n=n),
        grid_spec=grid_spec,
        out_shape=jax.ShapeDtypeStruct((n + 2 * tm, D_MODEL), F32),
        compiler_params=_cparams(("arbitrary",)),
        name="moe",
    )(e1, e2, src_g, src_g, src_s, xe, wg, wu, wd, wg, wu, wd)


def _residual_kernel(x1_ref, y_ref, *o_refs, starts):
    i = pl.program_id(0)
    ends = starts[1:] + (pl.num_programs(0),)
    for o_ref, end in zip(o_refs, ends):
        @pl.when(i < end)
        def _():
            o_ref[...] = x1_ref[...] + y_ref[...]


def _residual(x1, y, tm, row_splits):
    row = lambda i: (i, 0)
    outs = [jax.ShapeDtypeStruct((r, D_MODEL), F32) for r in row_splits]
    ospecs, starts, ntiles = _row_source_specs(outs, tm)
    return pl.pallas_call(
        functools.partial(_residual_kernel, starts=starts),
        grid=(ntiles,),
        in_specs=[pl.BlockSpec((tm, D_MODEL), row), pl.BlockSpec((tm, D_MODEL), row)],
        out_specs=ospecs,
        out_shape=outs,
        compiler_params=_cparams(("arbitrary",)),
        name="moe_residual",
    )(x1, y)


def _tiles(nb, s):
    return dict(tm=min(512, s), tconv=min(512, s),
                tq=min(512, s), tk=min(2048, s), tmoe=min(256, s))


def _layer(xsrcs, nb, s, p, l, tables, tl, row_splits):
    n = nb * s
    pa, pb, pc, pd = _inproj(xsrcs, p["norm1_g"][l], p["w_in"][l], tl["tm"])
    pa = pa.reshape(nb, s, SZ_A)
    pb = pb.reshape(nb, s, SZ_B)
    pc = pc.reshape(nb, s, SZ_C)
    pd = pd.reshape(nb, s, SZ_D)
    ua, ub = _conv_mixers(pa, pb, p["conv_a_w"][l], p["conv_a_b"][l], p["ln_a_g"][l],
                          p["ln_a_b"][l], p["conv_b_w"][l], tl["tconv"])
    gains = jnp.stack([jnp.tile(p[k][l], N_HEADS) for k in ("qn_c", "kn_c", "qn_d", "kn_d")])
    c1, c4, c16, qd, kd, vd = _prep(pc, pd, gains, tables, tl["tq"])
    uc = _dilated_mixture((c1, c4, c16), tl["tm"])
    ud = _gqa(qd, kd, vd, tl["tq"], tl["tk"])
    x1, xe = _outproj(ua.reshape(n, W_MIX), ub.reshape(n, W_MIX), uc.reshape(n, W_MIX),
                      ud.reshape(n, W_MIX), xsrcs, p["w_out"][l], p["norm2_g"][l],
                      p["w_route"][l], p["b_route"][l], tl["tm"])
    y = _moe(xe, p["w_gate"][l], p["w_up"][l], p["w_down"][l], tl["tmoe"])
    return _residual(x1, y, tl["tm"], row_splits)


def _trunk(xs, p):
    s = xs[0].shape[1]
    assert all(x.shape[1:] == (s, D_MODEL) for x in xs)
    nb = sum(x.shape[0] for x in xs)
    tables = _rope_tables(s)
    tl = _tiles(nb, s)
    depth = p["w_in"].shape[0]
    srcs = [x.reshape(-1, D_MODEL) for x in xs]
    for l in range(depth):
        splits = [x.shape[0] * s for x in xs] if l == depth - 1 else [nb * s]
        srcs = _layer(srcs, nb, s, p, l, tables, tl, splits)
    return tuple(y.reshape(x.shape) for y, x in zip(srcs, xs))


def _pack_params(norm1_g, w_in, conv_a_w, conv_a_b, ln_a_g, ln_a_b, conv_b_w, qn_c, kn_c, qn_d, kn_d,
                 w_out, norm2_g, w_group, b_group, w_router, b_router, w_gate, w_up, w_down):
    depth = w_in.shape[0]
    pad = LANES - N_GROUPS - N_EXPERTS
    w_route = jnp.concatenate([w_group, w_router, jnp.zeros((depth, D_MODEL, pad), F32)], axis=-1)
    b_route = jnp.concatenate([b_group, b_router, jnp.zeros((depth, pad), F32)], axis=-1)
    return dict(norm1_g=norm1_g, w_in=w_in.astype(BF16), conv_a_w=conv_a_w, conv_a_b=conv_a_b,
                ln_a_g=ln_a_g, ln_a_b=ln_a_b, conv_b_w=conv_b_w, qn_c=qn_c, kn_c=kn_c, qn_d=qn_d,
                kn_d=kn_d, w_out=w_out.astype(BF16), norm2_g=norm2_g,
                w_route=w_route.astype(BF16), b_route=b_route.reshape(depth, 1, LANES),
                w_gate=w_gate.astype(BF16), w_up=w_up.astype(BF16), w_down=w_down.astype(BF16))


def kernel(x_prompt, x_sample, norm1_g, w_in, conv_a_w, conv_a_b, ln_a_g, ln_a_b, conv_b_w, qn_c, kn_c, qn_d, kn_d, w_out, norm2_g, w_group, b_group, w_router, b_router, w_gate, w_up, w_down):
    p = _pack_params(norm1_g, w_in, conv_a_w, conv_a_b, ln_a_g, ln_a_b, conv_b_w, qn_c, kn_c, qn_d,
                     kn_d, w_out, norm2_g, w_group, b_group, w_router, b_router, w_gate, w_up, w_down)
    return _trunk((x_prompt, x_sample), p)
```

```python
import functools
import math

import jax
import jax.numpy as jnp
from jax import lax
from jax.experimental import pallas as pl
from jax.experimental.pallas import tpu as pltpu

D_MODEL = 1024
HEAD_DIM = 64
W_MIX = 256
N_HEADS = 4
N_KV_HEADS = 2
CONV_A_WIDTH = 31
CONV_B_WIDTH = 3
DIL_PATTERNS = ((128, 1), (512, 4), (2048, 16))
Q_BLOCK = 128
GRID_W = 64
ROPE_THETA = 10000.0
N_GROUPS = 4
EXPERTS_PER_GROUP = 4
N_EXPERTS = 16
PAIRS_PER_GROUP = EXPERTS_PER_GROUP * (EXPERTS_PER_GROUP - 1) // 2
N_CLASSES = N_GROUPS * PAIRS_PER_GROUP
D_EXPERT = 512
EPS = 1e-6
NEG_INF = -1e30
LOG2E = math.log2(math.e)

SZ_A, SZ_B, SZ_C, SZ_D = 512, 768, 768, 512
D_IN = SZ_A + SZ_B + SZ_C + SZ_D
LANES = 128
SUBLANES = 8
D_ROW = D_MODEL + LANES
HALO = 16
VMEM_LIMIT = 56 * 1024 * 1024

F32 = jnp.float32
BF16 = jnp.bfloat16


def _cparams(sem):
    return pltpu.CompilerParams(dimension_semantics=sem, vmem_limit_bytes=VMEM_LIMIT)


def _tile_of_source(i, *_, start, count):
    return (jnp.clip(i - start, 0, count - 1), 0)


def _row_source_specs(srcs, tm):
    specs, starts, start = [], [], 0
    for a in srcs:
        count = a.shape[0] // tm
        specs.append(pl.BlockSpec((tm, a.shape[1]),
                                  functools.partial(_tile_of_source, start=start, count=count)))
        starts.append(start)
        start += count
    return specs, tuple(starts), start


def _pick_source(refs, starts, i):
    x = refs[0][...]
    for ref, start in zip(refs[1:], starts[1:]):
        x = jnp.where(i >= start, ref[...], x)
    return x


def _inproj_kernel(*refs, starts):
    ns = len(starts)
    (g_ref, w_ref, gains_ref, cosc_ref, sinc_ref, cosd_ref, sind_ref, ones_ref,
     pa_ref, pb_ref, c1_ref, c4_ref, c16_ref, qd_ref, kd_ref, vd_ref, qkv_scr) = refs[ns:]
    x = _pick_source(refs[:ns], starts, pl.program_id(0))
    ms = jnp.mean(x * x, axis=-1, keepdims=True)
    h = (x * lax.rsqrt(ms + EPS) * g_ref[...]).astype(BF16)

    def proj(lo, sz):
        return jnp.dot(h, w_ref[:, lo:lo + sz], preferred_element_type=F32)

    pa_ref[...] = proj(0, SZ_A)
    pb_ref[...] = proj(SZ_A, SZ_B)
    _prep_rows(proj(SZ_A + SZ_B, SZ_C), proj(SZ_A + SZ_B + SZ_C, SZ_D), gains_ref[...],
               cosc_ref[...], sinc_ref[...], cosd_ref[...], sind_ref[...], ones_ref[...],
               (c1_ref, c4_ref, c16_ref), qd_ref, kd_ref, vd_ref, qkv_scr)


def _inproj(xsrcs, g, w_bf16, gains, tables, nb, s, tm):
    xspecs, starts, ntiles = _row_source_specs(xsrcs, tm)
    n = ntiles * tm
    tps = s // tm
    cos_c, sin_c, cos_d, sin_d = tables
    head = lax.broadcasted_iota(jnp.int32, (W_MIX, W_MIX), 0) // HEAD_DIM
    ones_bd = (head == head.T).astype(BF16)
    group = N_HEADS // N_KV_HEADS
    const = lambda i: (0, 0)
    row = lambda i: (i, 0)
    tab = lambda i: (i % tps, 0)
    hm = lambda i: (i // tps, 0, i % tps, 0)
    return pl.pallas_call(
        functools.partial(_inproj_kernel, starts=starts),
        grid=(ntiles,),
        in_specs=xspecs + [pl.BlockSpec((1, D_MODEL), const),
                           pl.BlockSpec((D_MODEL, D_IN), const),
                           pl.BlockSpec((4, W_MIX), const),
                           pl.BlockSpec((tm, W_MIX), tab),
                           pl.BlockSpec((tm, W_MIX), tab),
                           pl.BlockSpec((tm, W_MIX), tab),
                           pl.BlockSpec((tm, W_MIX), tab),
                           pl.BlockSpec((W_MIX, W_MIX), const)],
        out_specs=[pl.BlockSpec((tm, SZ_A), row), pl.BlockSpec((tm, SZ_B), row)]
        + [pl.BlockSpec((None, dil, tm // dil, SZ_C), hm) for _, dil in DIL_PATTERNS]
        + [pl.BlockSpec((None, N_KV_HEADS, None, group, tm, HEAD_DIM),
                        lambda i: (i // tps, 0, i % tps, 0, 0, 0)),
           pl.BlockSpec((None, N_KV_HEADS, HEAD_DIM, tm), lambda i: (i // tps, 0, 0, i % tps)),
           pl.BlockSpec((None, N_KV_HEADS, tm, LANES), hm)],
        out_shape=[jax.ShapeDtypeStruct((n, SZ_A), F32), jax.ShapeDtypeStruct((n, SZ_B), F32)]
        + [jax.ShapeDtypeStruct((nb, dil, s // dil, SZ_C), BF16) for _, dil in DIL_PATTERNS]
        + [jax.ShapeDtypeStruct((nb, N_KV_HEADS, tps, group, tm, HEAD_DIM), BF16),
           jax.ShapeDtypeStruct((nb, N_KV_HEADS, HEAD_DIM, s), BF16),
           jax.ShapeDtypeStruct((nb, N_KV_HEADS, s, LANES), BF16)],
        scratch_shapes=[pltpu.VMEM((SZ_C // LANES, tm, LANES), F32)],
        compiler_params=_cparams(("parallel",)),
        name="inproj",
    )(*xsrcs, g.reshape(1, D_MODEL), w_bf16, gains, cos_c, sin_c, cos_d, sin_d, ones_bd)


def _conv_kernel(pa_ref, pa_prev_ref, pa_next_ref, pb_ref, pb_prev_ref, pb_next_ref,
                 wa_ref, ba_ref, lg_ref, lb_ref, wb_ref, ua_ref, ub_ref, bufa, bufb, *, t):
    si = pl.program_id(1)
    ns = pl.num_programs(1)
    has_prev = (si > 0).astype(F32)
    has_next = (si < ns - 1).astype(F32)

    def glu(ref):
        p = ref[...]
        return p[:, :W_MIX] * jax.nn.sigmoid(p[:, W_MIX:])

    def gated(ref):
        p = ref[...]
        return p[:, W_MIX:2 * W_MIX] * p[:, 2 * W_MIX:]

    bufa[0:HALO, :] = glu(pa_prev_ref) * has_prev
    bufa[HALO:HALO + t, :] = glu(pa_ref)
    bufa[HALO + t:2 * HALO + t, :] = glu(pa_next_ref) * has_next
    bufb[0:HALO, :] = gated(pb_prev_ref) * has_prev
    bufb[HALO:HALO + t, :] = gated(pb_ref)
    bufb[HALO + t:2 * HALO + t, :] = gated(pb_next_ref) * has_next

    first = HALO - CONV_A_WIDTH // 2
    acc = jnp.zeros((t, W_MIX), F32)
    for b in range(SUBLANES):
        z = None
        for a in range((2 * HALO) // SUBLANES):
            k = SUBLANES * a + b - first
            if 0 <= k < CONV_A_WIDTH:
                term = bufa[SUBLANES * a:SUBLANES * a + t + SUBLANES, :] * wa_ref[k:k + 1, :]
                z = term if z is None else z + term
        acc = acc + z[b:b + t, :]
    acc = acc + ba_ref[...]
    mu = jnp.mean(acc, axis=-1, keepdims=True)
    xc = acc - mu
    var = jnp.mean(xc * xc, axis=-1, keepdims=True)
    y = xc * lax.rsqrt(var + EPS) * lg_ref[...] + lb_ref[...]
    ua_ref[...] = (y * jax.nn.sigmoid(y)).astype(BF16)

    accb = jnp.zeros((t, W_MIX), F32)
    for k in range(CONV_B_WIDTH):
        start = HALO - CONV_B_WIDTH // 2 + k
        accb = accb + bufb[start:start + t, :] * wb_ref[k:k + 1, :]
    ub_ref[...] = (pb_ref[:, :W_MIX] * accb).astype(BF16)


def _conv_mixers(pa, pb, wa, ba, lg, lb, wb, t):
    nb, s, _ = pa.shape
    hb = t // HALO
    last = s // HALO - 1

    def main(b, i):
        return (b, i, 0)

    def prev(b, i):
        return (b, jnp.maximum(i * hb - 1, 0), 0)

    def nxt(b, i):
        return (b, jnp.minimum((i + 1) * hb, last), 0)

    const = lambda b, i: (0, 0)
    return pl.pallas_call(
        functools.partial(_conv_kernel, t=t),
        grid=(nb, s // t),
        in_specs=[pl.BlockSpec((None, t, SZ_A), main),
                  pl.BlockSpec((None, HALO, SZ_A), prev),
                  pl.BlockSpec((None, HALO, SZ_A), nxt),
                  pl.BlockSpec((None, t, SZ_B), main),
                  pl.BlockSpec((None, HALO, SZ_B), prev),
                  pl.BlockSpec((None, HALO, SZ_B), nxt),
                  pl.BlockSpec((CONV_A_WIDTH, W_MIX), const),
                  pl.BlockSpec((1, W_MIX), const),
                  pl.BlockSpec((1, W_MIX), const),
                  pl.BlockSpec((1, W_MIX), const),
                  pl.BlockSpec((CONV_B_WIDTH, W_MIX), const)],
        out_specs=[pl.BlockSpec((None, t, W_MIX), main),
                   pl.BlockSpec((None, t, W_MIX), main)],
        out_shape=[jax.ShapeDtypeStruct((nb, s, W_MIX), BF16)] * 2,
        scratch_shapes=[pltpu.VMEM((t + 2 * HALO, W_MIX), F32),
                        pltpu.VMEM((t + 2 * HALO, W_MIX), F32)],
        compiler_params=_cparams(("parallel", "parallel")),
        name="conv_mixers",
    )(pa, pa, pa, pb, pb, pb, wa, ba.reshape(1, W_MIX), lg.reshape(1, W_MIX),
      lb.reshape(1, W_MIX), wb)


def _head_sumsq(x, ones_bd):
    x2 = x * x
    hi = x2.astype(BF16)
    lo = (x2 - hi.astype(F32)).astype(BF16)
    return (jnp.dot(hi, ones_bd, preferred_element_type=F32)
            + jnp.dot(lo, ones_bd, preferred_element_type=F32))


def _rope(x, cos, sin_signed, half):
    w = x.shape[-1]
    lane = lax.broadcasted_iota(jnp.int32, x.shape, 1)
    fwd = pltpu.roll(x, w - half, 1)
    bwd = pltpu.roll(x, half, 1)
    rot = jnp.where(lane % (2 * half) < half, fwd, bwd)
    return x * cos + rot * sin_signed


def _prep_rows(pc, pd, gains, cosc, sinc, cosd, sind, ones_bd, c_refs, qd_ref, kd_ref, vd_ref,
               qkv_scr):
    scale = HEAD_DIM ** -0.5

    def normed(x, g, ones):
        ss = _head_sumsq(x, ones)
        return x * lax.rsqrt(ss * (1.0 / HEAD_DIM) + EPS) * g

    qc = _rope(normed(pc[:, 0:256], gains[0:1, :], ones_bd), cosc, sinc, HEAD_DIM // 2) * scale
    kc = _rope(normed(pc[:, 256:512], gains[1:2, :], ones_bd), cosc, sinc, HEAD_DIM // 2)
    qkv = (qc, kc, pc[:, 512:768])
    ncb = SZ_C // LANES
    for c in range(ncb):
        part = qkv[c * LANES // W_MIX]
        lo = c * LANES % W_MIX
        qkv_scr[c] = part[:, lo:lo + LANES]
    ts = pc.shape[0]
    for ref, (_, dil) in zip(c_refs, DIL_PATTERNS):
        for r in range(dil):
            for c in range(ncb):
                ref[r, :, c * LANES:(c + 1) * LANES] = (
                    qkv_scr[c, pl.ds(r, ts // dil, stride=dil), :].astype(BF16))

    qd = _rope(normed(pd[:, 0:256], gains[2:3, :], ones_bd), cosd, sind, HEAD_DIM // 4) * (scale * LOG2E)
    kd = _rope(normed(pd[:, 256:384], gains[3:4, 0:128], ones_bd[0:128, 0:128]),
               cosd[:, 0:128], sind[:, 0:128], HEAD_DIM // 4)
    vd = pd[:, 384:512]
    group = N_HEADS // N_KV_HEADS
    for h in range(N_HEADS):
        qd_ref[h // group, h % group] = qd[:, h * HEAD_DIM:(h + 1) * HEAD_DIM].astype(BF16)
    lane = lax.broadcasted_iota(jnp.int32, vd.shape, 1)
    kd_t = kd.T.astype(BF16)
    for h in range(N_KV_HEADS):
        kd_ref[h] = kd_t[h * HEAD_DIM:(h + 1) * HEAD_DIM, :]
        vh = vd if h == 0 else pltpu.roll(vd, LANES - h * HEAD_DIM, 1)
        vd_ref[h] = jnp.where(lane < HEAD_DIM, vh, jnp.where(lane == HEAD_DIM, 1.0, 0.0)).astype(BF16)


def _rope_tables(s):
    def angles(pos, dim):
        inv = ROPE_THETA ** (-jnp.arange(0, dim, 2, dtype=F32) / dim)
        return pos.astype(F32)[:, None] * inv[None, :]

    pos = jnp.arange(s)
    ang = angles(pos, HEAD_DIM)
    cos_c = jnp.tile(jnp.concatenate([jnp.cos(ang), jnp.cos(ang)], -1), (1, N_HEADS))
    sin_c = jnp.tile(jnp.concatenate([-jnp.sin(ang), jnp.sin(ang)], -1), (1, N_HEADS))
    row = pos // GRID_W
    col = pos % GRID_W
    ar = angles(row, HEAD_DIM // 2)
    ac = angles(col, HEAD_DIM // 2)
    cos_d = jnp.tile(jnp.concatenate([jnp.cos(ar), jnp.cos(ar), jnp.cos(ac), jnp.cos(ac)], -1),
                     (1, N_HEADS))
    sin_d = jnp.tile(jnp.concatenate([-jnp.sin(ar), jnp.sin(ar), -jnp.sin(ac), jnp.sin(ac)], -1),
                     (1, N_HEADS))
    return cos_c, sin_c, cos_d, sin_d


def _dilated_kernel(q_ref, k_ref, v_ref, o_ref, lse_ref, *, seq_sub, half, lq):
    span = Q_BLOCK + 2 * half
    qi = pl.program_id(2)
    nt = (((1,), (1,)), ((), ()))
    for i in range(lq // Q_BLOCK):
        q0 = i * Q_BLOCK
        qabs = qi * lq + q0
        ks = pl.multiple_of(jnp.clip(qabs - half, 0, seq_sub - span), half)
        q = q_ref[q0:q0 + Q_BLOCK, :]
        k = k_ref[pl.ds(ks, span), :]
        v = v_ref[pl.ds(ks, span), :]
        qpos = qabs + lax.broadcasted_iota(jnp.int32, (Q_BLOCK, span), 0)
        kpos = ks + lax.broadcasted_iota(jnp.int32, (Q_BLOCK, span), 1)
        mask = jnp.abs(kpos - qpos) <= half
        o_parts, l_parts = [], []
        for h in range(N_HEADS):
            hs = slice(h * HEAD_DIM, (h + 1) * HEAD_DIM)
            sc = lax.dot_general(q[:, hs], k[:, hs], nt, preferred_element_type=F32)
            sc = jnp.where(mask, sc, NEG_INF)
            m = jnp.max(sc, axis=-1, keepdims=True)
            p = jnp.exp(sc - m)
            l = jnp.sum(p, axis=-1, keepdims=True)
            num = jnp.dot(p.astype(BF16), v[:, hs], preferred_element_type=F32)
            o_parts.append(num / l)
            l_parts.append(jnp.broadcast_to(m + jnp.log(l), (Q_BLOCK, HEAD_DIM)))
        o_ref[q0:q0 + Q_BLOCK, :] = jnp.concatenate(o_parts, axis=-1)
        lse_ref[q0:q0 + Q_BLOCK, :] = jnp.concatenate(l_parts, axis=-1)


def _dilated_pattern(qkv, win, dil):
    nb, _, seq_sub, _ = qkv.shape
    half = win // (2 * dil)
    lq = min(seq_sub, 512)
    qspec = pl.BlockSpec((None, None, lq, W_MIX), lambda b, r, i: (b, r, i, 0))
    kspec = pl.BlockSpec((None, None, seq_sub, W_MIX), lambda b, r, i: (b, r, 0, 1))
    vspec = pl.BlockSpec((None, None, seq_sub, W_MIX), lambda b, r, i: (b, r, 0, 2))
    return pl.pallas_call(
        functools.partial(_dilated_kernel, seq_sub=seq_sub, half=half, lq=lq),
        grid=(nb, dil, seq_sub // lq),
        in_specs=[qspec, kspec, vspec],
        out_specs=[qspec, qspec],
        out_shape=[jax.ShapeDtypeStruct((nb, dil, seq_sub, W_MIX), F32)] * 2,
        compiler_params=_cparams(("parallel", "parallel", "arbitrary")),
        name=f"dilated_w{win}_d{dil}",
    )(qkv, qkv, qkv)


def _mixture_kernel(*refs, ts):
    n = len(DIL_PATTERNS)
    o_refs, l_refs = refs[0:2 * n:2], refs[1:2 * n:2]
    uc_ref = refs[2 * n]
    scr = refs[2 * n + 1:]

    def natural(ref, dil, buf):
        if dil == 1:
            return ref[0]
        ncb = W_MIX // LANES
        for r in range(dil):
            for c in range(ncb):
                buf[c, pl.ds(r, ts // dil, stride=dil), :] = ref[r, :, c * LANES:(c + 1) * LANES]
        return jnp.concatenate([buf[c] for c in range(ncb)], axis=-1)

    os_, ls_ = [], []
    for j, (_, dil) in enumerate(DIL_PATTERNS):
        os_.append(natural(o_refs[j], dil, scr[2 * j]))
        ls_.append(natural(l_refs[j], dil, scr[2 * j + 1]))
    mx = functools.reduce(jnp.maximum, ls_)
    ws = [jnp.exp(l - mx) for l in ls_]
    num = sum(w * o for w, o in zip(ws, os_))
    uc_ref[...] = (num / sum(ws)).astype(uc_ref.dtype)


def _dilated_mixture(c_layouts, ts):
    nb, _, s, _ = c_layouts[0].shape
    ins, in_specs = [], []
    for qkv, (win, dil) in zip(c_layouts, DIL_PATTERNS):
        o, lse = _dilated_pattern(qkv, win, dil)
        spec = pl.BlockSpec((None, dil, ts // dil, W_MIX), lambda b, i: (b, 0, i, 0))
        ins += [o, lse]
        in_specs += [spec, spec]
    return pl.pallas_call(
        functools.partial(_mixture_kernel, ts=ts),
        grid=(nb, s // ts),
        in_specs=in_specs,
        out_specs=pl.BlockSpec((None, ts, W_MIX), lambda b, i: (b, i, 0)),
        out_shape=jax.ShapeDtypeStruct((nb, s, W_MIX), BF16),
        scratch_shapes=[pltpu.VMEM((W_MIX // LANES, ts, LANES), F32)] * (2 * len(DIL_PATTERNS)),
        compiler_params=_cparams(("parallel", "parallel")),
        name="dilated_mixture",
    )(*ins)


def _gqa_kernel(q_ref, k_ref, v_ref, o_ref, *, tq, tk, s):
    group = N_HEADS // N_KV_HEADS
    rows = group * tq
    qs = [q_ref[g] for g in range(N_KV_HEADS)]

    def body(c, carry):
        k0 = pl.multiple_of(c * tk, tk)
        new = []
        for g in range(N_KV_HEADS):
            kt = k_ref[g, :, pl.ds(k0, tk)]
            v = v_ref[g, pl.ds(k0, tk), :]
            m, acc = carry[g]
            sc = jnp.dot(qs[g], kt, preferred_element_type=F32)
            m_new = jnp.maximum(m, jnp.max(sc, axis=-1, keepdims=True))
            alpha = jnp.exp2(m - m_new)
            p = jnp.exp2(sc - m_new).astype(BF16)
            acc = alpha * acc + jnp.dot(p, v, preferred_element_type=F32)
            new.append((m_new, acc))
        return tuple(new)

    init = tuple((jnp.full((rows, 1), -jnp.inf, F32), jnp.zeros((rows, LANES), F32))
                 for _ in range(N_KV_HEADS))
    fin = lax.fori_loop(0, s // tk, body, init)
    outs = []
    for (_, acc) in fin:
        o = acc[:, :HEAD_DIM] / acc[:, HEAD_DIM:HEAD_DIM + 1]
        outs += [o[j * tq:(j + 1) * tq, :] for j in range(group)]
    o_ref[...] = jnp.concatenate(outs, axis=-1).astype(o_ref.dtype)


def _gqa(qd, kd, vd, tq, tk):
    nb, _, nq, group, _, _ = qd.shape
    s = nq * tq
    qd = qd.reshape(nb, N_KV_HEADS, nq, group * tq, HEAD_DIM)
    return pl.pallas_call(
        functools.partial(_gqa_kernel, tq=tq, tk=tk, s=s),
        grid=(nb, nq),
        in_specs=[pl.BlockSpec((None, N_KV_HEADS, None, group * tq, HEAD_DIM),
                               lambda b, i: (b, 0, i, 0, 0)),
                  pl.BlockSpec((None, N_KV_HEADS, HEAD_DIM, s), lambda b, i: (b, 0, 0, 0)),
                  pl.BlockSpec((None, N_KV_HEADS, s, LANES), lambda b, i: (b, 0, 0, 0))],
        out_specs=pl.BlockSpec((None, tq, W_MIX), lambda b, i: (b, i, 0)),
        out_shape=jax.ShapeDtypeStruct((nb, s, W_MIX), BF16),
        compiler_params=_cparams(("parallel", "arbitrary")),
        name="gqa",
    )(qd, kd, vd)


def _outproj_kernel(*refs, starts):
    ns = len(starts)
    ua_ref, ub_ref, uc_ref, ud_ref = refs[:4]
    w_ref, g_ref, wr_ref, br_ref, x1_ref, xe_ref = refs[4 + ns:]
    acc = _pick_source(refs[4:4 + ns], starts, pl.program_id(0))
    for j, ref in enumerate((ua_ref, ub_ref, uc_ref, ud_ref)):
        acc = acc + jnp.dot(ref[...], w_ref[j * W_MIX:(j + 1) * W_MIX, :],
                            preferred_element_type=F32)
    x1_ref[...] = acc
    ms = jnp.mean(acc * acc, axis=-1, keepdims=True)
    xn = acc * lax.rsqrt(ms + EPS) * g_ref[...]
    logits = jnp.dot(xn.astype(BF16), wr_ref[...], preferred_element_type=F32) + br_ref[...]
    xe_ref[:, :D_MODEL] = xn
    xe_ref[:, D_MODEL:] = _route(logits)


def _route(logits):
    lane = lax.broadcasted_iota(jnp.int32, logits.shape, 1)
    big = jnp.int32(LANES)
    ninf = -jnp.inf

    def argmax(vals):
        mx = jnp.max(vals, axis=-1, keepdims=True)
        idx = jnp.min(jnp.where(vals == mx, lane, big), axis=-1, keepdims=True)
        return mx, idx

    glog = jnp.where(lane < N_GROUPS, logits, ninf)
    gmax, gidx = argmax(glog)
    g_w = 1.0 / jnp.sum(jnp.exp(glog - gmax), axis=-1, keepdims=True)
    lo = N_GROUPS + gidx * EXPERTS_PER_GROUP
    el = jnp.where((lane >= lo) & (lane < lo + EXPERTS_PER_GROUP), logits, ninf)
    t1, i1 = argmax(el)
    el2 = jnp.where(lane == i1, ninf, el)
    t2, i2 = argmax(el2)
    e2 = jnp.exp(t2 - t1)
    w1 = 1.0 / (1.0 + e2)
    w2 = e2 * w1
    gates = jnp.where(lane == i1, w1 * g_w, jnp.where(lane == i2, w2 * g_w, 0.0))
    a = jnp.minimum(i1, i2) - lo
    b = jnp.maximum(i1, i2) - lo
    pair = lax.shift_right_logical(a * (2 * EXPERTS_PER_GROUP - 1 - a), 1) + (b - a - 1)
    cls = gidx * PAIRS_PER_GROUP + pair
    return jnp.where(lane == 0, cls.astype(F32), gates)


def _outproj(ua, ub, uc, ud, xsrcs, w_bf16, g2, wr, br, tm):
    xspecs, starts, ntiles = _row_source_specs(xsrcs, tm)
    n = ntiles * tm
    row = lambda i: (i, 0)
    const = lambda i: (0, 0)
    return pl.pallas_call(
        functools.partial(_outproj_kernel, starts=starts),
        grid=(ntiles,),
        in_specs=[pl.BlockSpec((tm, W_MIX), row)] * 4 + xspecs
        + [pl.BlockSpec((D_MODEL, D_MODEL), const),
           pl.BlockSpec((1, D_MODEL), const),
           pl.BlockSpec((D_MODEL, LANES), const),
           pl.BlockSpec((1, LANES), const)],
        out_specs=[pl.BlockSpec((tm, D_MODEL), row),
                   pl.BlockSpec((tm, D_ROW), row)],
        out_shape=[jax.ShapeDtypeStruct((n, D_MODEL), F32),
                   jax.ShapeDtypeStruct((n, D_ROW), F32)],
        compiler_params=_cparams(("parallel",)),
        name="outproj_router",
    )(ua, ub, uc, ud, *xsrcs, w_bf16, g2.reshape(1, D_MODEL), wr, br)


def _class_experts():
    e1, e2 = [], []
    for g in range(N_GROUPS):
        for a in range(EXPERTS_PER_GROUP):
            for b in range(a + 1, EXPERTS_PER_GROUP):
                e1.append(g * EXPERTS_PER_GROUP + a)
                e2.append(g * EXPERTS_PER_GROUP + b)
    return jnp.array(e1, jnp.int32), jnp.array(e2, jnp.int32)


def _moe_plan(cls, tm):
    n = cls.shape[0]
    nt = n // tm + N_CLASSES
    classes = jnp.arange(N_CLASSES, dtype=jnp.int32)
    counts = jnp.sum((cls[:, None] == classes[None, :]).astype(jnp.int32), axis=0)
    _, order = lax.sort((cls, jnp.arange(n, dtype=jnp.int32)), num_keys=1, is_stable=True)
    cstart = jnp.cumsum(counts) - counts
    ctiles = (counts + tm - 1) // tm
    tstart = jnp.cumsum(ctiles) - ctiles
    t = jnp.arange(nt, dtype=jnp.int32)
    tclass = jnp.clip(jnp.sum((t[:, None] >= tstart[None, :]).astype(jnp.int32), axis=1) - 1,
                      0, N_CLASSES - 1)
    p = jnp.arange(nt * tm, dtype=jnp.int32)
    room = N_CLASSES * tm
    ext = jnp.concatenate([jnp.zeros((room,), jnp.int32), order, jnp.zeros((room,), jnp.int32)])
    src = jnp.full((nt * tm,), -1, jnp.int32)
    for c in range(N_CLASSES):
        lo = tstart[c] * tm
        shifted = lax.dynamic_slice(ext, (room - (lo - cstart[c]),), (nt * tm,))
        src = jnp.where((p >= lo) & (p < lo + counts[c]), shifted, src)
    trash = n + p % (2 * tm)
    src_g = jnp.where(src < 0, 0, src).reshape(nt, 1, tm)
    src_s = jnp.where(src < 0, trash, src).reshape(nt, 1, tm)
    e1, e2 = _class_experts()
    return e1[tclass], e2[tclass], src_g, src_s


def _moe_kernel(e1_ref, e2_ref, srcg_cur_ref, srcg_nxt_ref, srcs_ref, xe_hbm,
                wg1_ref, wu1_ref, wd1_ref, wg2_ref, wu2_ref, wd2_ref,
                y_hbm, xg, ys, gsem, ssem, *, tm, n):
    t = pl.program_id(0)
    nt = pl.num_programs(0)
    slot = t % 2

    def row_gather(src_ref, dst_slot):
        for r in range(tm):
            tok = src_ref[0, r]
            pltpu.make_async_copy(xe_hbm.at[tok], xg.at[dst_slot, r],
                                  gsem.at[dst_slot]).start(priority=r % 2)

    def tile_gather_wait(s_):
        pltpu.make_async_copy(xe_hbm.at[pl.ds(0, tm)], xg.at[s_], gsem.at[s_]).wait()

    def tile_scatter_wait(s_):
        pltpu.make_async_copy(ys.at[s_], y_hbm.at[pl.ds(0, tm)], ssem.at[s_]).wait()

    @pl.when(t == 0)
    def _():
        row_gather(srcg_cur_ref, 0)
        ys[1] = jnp.zeros((tm, D_MODEL), F32)
        for half in range(2):
            cp = pltpu.make_async_copy(ys.at[1], y_hbm.at[pl.ds(n + half * tm, tm)], ssem.at[1])
            cp.start()
            cp.wait()

    tile_gather_wait(slot)

    @pl.when(t >= 2)
    def _():
        tile_scatter_wait(slot)

    row_gather(srcg_nxt_ref, 1 - slot)

    xe = xg[slot]
    x = xe[:, :D_MODEL].astype(BF16)
    gates = xe[:, D_MODEL:]
    lane = lax.broadcasted_iota(jnp.int32, gates.shape, 1)
    y = jnp.zeros((tm, D_MODEL), F32)
    for e_ref, wg_ref, wu_ref, wd_ref in ((e1_ref, wg1_ref, wu1_ref, wd1_ref),
                                          (e2_ref, wg2_ref, wu2_ref, wd2_ref)):
        ge = jnp.sum(jnp.where(lane == N_GROUPS + e_ref[t], gates, 0.0), axis=-1, keepdims=True)
        a = jnp.dot(x, wg_ref[...], preferred_element_type=F32)
        u = jnp.dot(x, wu_ref[...], preferred_element_type=F32)
        hdn = (a * jax.nn.sigmoid(a) * u).astype(BF16)
        y = y + ge * jnp.dot(hdn, wd_ref[...], preferred_element_type=F32)
    ys[slot] = y

    for r in range(tm):
        tok = srcs_ref[0, r]
        pltpu.make_async_copy(ys.at[slot, r], y_hbm.at[tok], ssem.at[slot]).start(priority=r % 2)

    @pl.when(t == nt - 1)
    def _():
        tile_gather_wait(1 - slot)
        tile_scatter_wait(slot)

        @pl.when(nt >= 2)
        def _():
            tile_scatter_wait(1 - slot)


def _moe(xe, wg, wu, wd, tm):
    n = xe.shape[0]
    cls = xe[:, D_MODEL].astype(jnp.int32)
    e1, e2, src_g, src_s = _moe_plan(cls, tm)
    nt = e1.shape[0]
    smem = lambda imap: pl.BlockSpec((None, 1, tm), imap, memory_space=pltpu.SMEM)
    w1 = lambda shp: pl.BlockSpec((None,) + shp, lambda t, e1, e2: (e1[t], 0, 0))
    w2 = lambda shp: pl.BlockSpec((None,) + shp, lambda t, e1, e2: (e2[t], 0, 0))
    up, down = (D_MODEL, D_EXPERT), (D_EXPERT, D_MODEL)
    grid_spec = pltpu.PrefetchScalarGridSpec(
        num_scalar_prefetch=2,
        grid=(nt,),
        in_specs=[smem(lambda t, e1, e2: (t, 0, 0)),
                  smem(lambda t, e1, e2: (jnp.minimum(t + 1, nt - 1), 0, 0)),
                  smem(lambda t, e1, e2: (t, 0, 0)),
                  pl.BlockSpec(memory_space=pl.ANY),
                  w1(up), w1(up), w1(down), w2(up), w2(up), w2(down)],
        out_specs=pl.BlockSpec(memory_space=pl.ANY),
        scratch_shapes=[pltpu.VMEM((2, tm, D_ROW), F32),
                        pltpu.VMEM((2, tm, D_MODEL), F32),
                        pltpu.SemaphoreType.DMA((2,)),
                        pltpu.SemaphoreType.DMA((2,))])
    return pl.pallas_call(
        functools.partial(_moe_kernel, tm=tm, n=n),
        grid_spec=grid_spec,
        out_shape=jax.ShapeDtypeStruct((n + 2 * tm, D_MODEL), F32),
        compiler_params=_cparams(("arbitrary",)),
        name="moe",
    )(e1, e2, src_g, src_g, src_s, xe, wg, wu, wd, wg, wu, wd)


def _residual_kernel(x1_ref, y_ref, *o_refs, starts):
    i = pl.program_id(0)
    ends = starts[1:] + (pl.num_programs(0),)
    for o_ref, end in zip(o_refs, ends):
        @pl.when(i < end)
        def _():
            o_ref[...] = x1_ref[...] + y_ref[...]


def _residual(x1, y, tm, row_splits):
    row = lambda i: (i, 0)
    outs = [jax.ShapeDtypeStruct((r, D_MODEL), F32) for r in row_splits]
    ospecs, starts, ntiles = _row_source_specs(outs, tm)
    return pl.pallas_call(
        functools.partial(_residual_kernel, starts=starts),
        grid=(ntiles,),
        in_specs=[pl.BlockSpec((tm, D_MODEL), row), pl.BlockSpec((tm, D_MODEL), row)],
        out_specs=ospecs,
        out_shape=outs,
        compiler_params=_cparams(("arbitrary",)),
        name="moe_residual",
    )(x1, y)


def _tiles(nb, s):
    return dict(tm=min(512, s), tconv=min(512, s),
                tq=min(512, s), tk=min(2048, s), tmoe=min(256, s))


def _layer(xsrcs, nb, s, p, l, tables, tl, row_splits):
    n = nb * s
    gains = jnp.stack([jnp.tile(p[k][l], N_HEADS) for k in ("qn_c", "kn_c", "qn_d", "kn_d")])
    pa, pb, c1, c4, c16, qd, kd, vd = _inproj(xsrcs, p["norm1_g"][l], p["w_in"][l], gains, tables,
                                              nb, s, tl["tq"])
    pa = pa.reshape(nb, s, SZ_A)
    pb = pb.reshape(nb, s, SZ_B)
    ua, ub = _conv_mixers(pa, pb, p["conv_a_w"][l], p["conv_a_b"][l], p["ln_a_g"][l],
                          p["ln_a_b"][l], p["conv_b_w"][l], tl["tconv"])
    uc = _dilated_mixture((c1, c4, c16), tl["tm"])
    ud = _gqa(qd, kd, vd, tl["tq"], tl["tk"])
    x1, xe = _outproj(ua.reshape(n, W_MIX), ub.reshape(n, W_MIX), uc.reshape(n, W_MIX),
                      ud.reshape(n, W_MIX), xsrcs, p["w_out"][l], p["norm2_g"][l],
                      p["w_route"][l], p["b_route"][l], tl["tm"])
    y = _moe(xe, p["w_gate"][l], p["w_up"][l], p["w_down"][l], tl["tmoe"])
    return _residual(x1, y, tl["tm"], row_splits)


def _trunk(xs, p):
    s = xs[0].shape[1]
    assert all(x.shape[1:] == (s, D_MODEL) for x in xs)
    nb = sum(x.shape[0] for x in xs)
    tables = _rope_tables(s)
    tl = _tiles(nb, s)
    depth = p["w_in"].shape[0]
    srcs = [x.reshape(-1, D_MODEL) for x in xs]
    for l in range(depth):
        splits = [x.shape[0] * s for x in xs] if l == depth - 1 else [nb * s]
        srcs = _layer(srcs, nb, s, p, l, tables, tl, splits)
    return tuple(y.reshape(x.shape) for y, x in zip(srcs, xs))


def _pack_params(norm1_g, w_in, conv_a_w, conv_a_b, ln_a_g, ln_a_b, conv_b_w, qn_c, kn_c, qn_d, kn_d,
                 w_out, norm2_g, w_group, b_group, w_router, b_router, w_gate, w_up, w_down):
    depth = w_in.shape[0]
    pad = LANES - N_GROUPS - N_EXPERTS
    w_route = jnp.concatenate([w_group, w_router, jnp.zeros((depth, D_MODEL, pad), F32)], axis=-1)
    b_route = jnp.concatenate([b_group, b_router, jnp.zeros((depth, pad), F32)], axis=-1)
    return dict(norm1_g=norm1_g, w_in=w_in.astype(BF16), conv_a_w=conv_a_w, conv_a_b=conv_a_b,
                ln_a_g=ln_a_g, ln_a_b=ln_a_b, conv_b_w=conv_b_w, qn_c=qn_c, kn_c=kn_c, qn_d=qn_d,
                kn_d=kn_d, w_out=w_out.astype(BF16), norm2_g=norm2_g,
                w_route=w_route.astype(BF16), b_route=b_route.reshape(depth, 1, LANES),
                w_gate=w_gate.astype(BF16), w_up=w_up.astype(BF16), w_down=w_down.astype(BF16))


def kernel(x_prompt, x_sample, norm1_g, w_in, conv_a_w, conv_a_b, ln_a_g, ln_a_b, conv_b_w, qn_c, kn_c, qn_d, kn_d, w_out, norm2_g, w_group, b_group, w_router, b_router, w_gate, w_up, w_down):
    p = _pack_params(norm1_g, w_in, conv_a_w, conv_a_b, ln_a_g, ln_a_b, conv_b_w, qn_c, kn_c, qn_d,
                     kn_d, w_out, norm2_g, w_group, b_group, w_router, b_router, w_gate, w_up, w_down)
    return _trunk((x_prompt, x_sample), p)
```

```python
import functools
import math

import jax
import jax.numpy as jnp
from jax import lax
from jax.experimental import pallas as pl
from jax.experimental.pallas import tpu as pltpu

D_MODEL = 1024
HEAD_DIM = 64
W_MIX = 256
N_HEADS = 4
N_KV_HEADS = 2
CONV_A_WIDTH = 31
CONV_B_WIDTH = 3
DIL_PATTERNS = ((128, 1), (512, 4), (2048, 16))
Q_BLOCK = 128
GRID_W = 64
ROPE_THETA = 10000.0
N_GROUPS = 4
EXPERTS_PER_GROUP = 4
N_EXPERTS = 16
PAIRS_PER_GROUP = EXPERTS_PER_GROUP * (EXPERTS_PER_GROUP - 1) // 2
N_CLASSES = N_GROUPS * PAIRS_PER_GROUP
D_EXPERT = 512
EPS = 1e-6
NEG_INF = -1e30
LOG2E = math.log2(math.e)

SZ_A, SZ_B, SZ_C, SZ_D = 512, 768, 768, 512
D_IN = SZ_A + SZ_B + SZ_C + SZ_D
LANES = 128
SUBLANES = 8
D_ROW = D_MODEL + LANES
HALO = 16
VMEM_LIMIT = 56 * 1024 * 1024

F32 = jnp.float32
BF16 = jnp.bfloat16


def _cparams(sem):
    return pltpu.CompilerParams(dimension_semantics=sem, vmem_limit_bytes=VMEM_LIMIT)


def _tile_of_source(i, *_, start, count):
    return (jnp.clip(i - start, 0, count - 1), 0)


def _row_source_specs(srcs, tm):
    specs, starts, start = [], [], 0
    for a in srcs:
        count = a.shape[0] // tm
        specs.append(pl.BlockSpec((tm, a.shape[1]),
                                  functools.partial(_tile_of_source, start=start, count=count)))
        starts.append(start)
        start += count
    return specs, tuple(starts), start


def _pick_source(refs, starts, i):
    x = refs[0][...]
    for ref, start in zip(refs[1:], starts[1:]):
        x = jnp.where(i >= start, ref[...], x)
    return x


def _inproj_kernel(*refs, starts):
    ns = len(starts)
    (g_ref, w_ref, gains_ref, cosc_ref, sinc_ref, cosd_ref, sind_ref, ones_ref,
     pa_ref, pb_ref, c1_ref, c4_ref, c16_ref, qd_ref, kd_ref, vd_ref, qkv_scr) = refs[ns:]
    x = _pick_source(refs[:ns], starts, pl.program_id(0))
    ms = jnp.mean(x * x, axis=-1, keepdims=True)
    h = (x * lax.rsqrt(ms + EPS) * g_ref[...]).astype(BF16)

    def proj(lo, sz):
        return jnp.dot(h, w_ref[:, lo:lo + sz], preferred_element_type=F32)

    pa_ref[...] = proj(0, SZ_A)
    pb_ref[...] = proj(SZ_A, SZ_B)
    _prep_rows(proj(SZ_A + SZ_B, SZ_C), proj(SZ_A + SZ_B + SZ_C, SZ_D), gains_ref[...],
               cosc_ref[...], sinc_ref[...], cosd_ref[...], sind_ref[...], ones_ref[...],
               (c1_ref, c4_ref, c16_ref), qd_ref, kd_ref, vd_ref, qkv_scr)


def _inproj(xsrcs, g, w_bf16, gains, tables, nb, s, tm):
    xspecs, starts, ntiles = _row_source_specs(xsrcs, tm)
    n = ntiles * tm
    tps = s // tm
    cos_c, sin_c, cos_d, sin_d = tables
    head = lax.broadcasted_iota(jnp.int32, (W_MIX, W_MIX), 0) // HEAD_DIM
    ones_bd = (head == head.T).astype(BF16)
    group = N_HEADS // N_KV_HEADS
    const = lambda i: (0, 0)
    row = lambda i: (i, 0)
    tab = lambda i: (i % tps, 0)
    hm = lambda i: (i // tps, 0, i % tps, 0)
    return pl.pallas_call(
        functools.partial(_inproj_kernel, starts=starts),
        grid=(ntiles,),
        in_specs=xspecs + [pl.BlockSpec((1, D_MODEL), const),
                           pl.BlockSpec((D_MODEL, D_IN), const),
                           pl.BlockSpec((4, W_MIX), const),
                           pl.BlockSpec((tm, W_MIX), tab),
                           pl.BlockSpec((tm, W_MIX), tab),
                           pl.BlockSpec((tm, W_MIX), tab),
                           pl.BlockSpec((tm, W_MIX), tab),
                           pl.BlockSpec((W_MIX, W_MIX), const)],
        out_specs=[pl.BlockSpec((tm, SZ_A), row), pl.BlockSpec((tm, SZ_B), row)]
        + [pl.BlockSpec((None, dil, tm // dil, SZ_C), hm) for _, dil in DIL_PATTERNS]
        + [pl.BlockSpec((None, N_KV_HEADS, None, group, tm, HEAD_DIM),
                        lambda i: (i // tps, 0, i % tps, 0, 0, 0)),
           pl.BlockSpec((None, N_KV_HEADS, HEAD_DIM, tm), lambda i: (i // tps, 0, 0, i % tps)),
           pl.BlockSpec((None, N_KV_HEADS, tm, LANES), hm)],
        out_shape=[jax.ShapeDtypeStruct((n, SZ_A), F32), jax.ShapeDtypeStruct((n, SZ_B), F32)]
        + [jax.ShapeDtypeStruct((nb, dil, s // dil, SZ_C), BF16) for _, dil in DIL_PATTERNS]
        + [jax.ShapeDtypeStruct((nb, N_KV_HEADS, tps, group, tm, HEAD_DIM), BF16),
           jax.ShapeDtypeStruct((nb, N_KV_HEADS, HEAD_DIM, s), BF16),
           jax.ShapeDtypeStruct((nb, N_KV_HEADS, s, LANES), BF16)],
        scratch_shapes=[pltpu.VMEM((SZ_C // LANES, tm, LANES), F32)],
        compiler_params=_cparams(("parallel",)),
        name="inproj",
    )(*xsrcs, g.reshape(1, D_MODEL), w_bf16, gains, cos_c, sin_c, cos_d, sin_d, ones_bd)


def _conv_kernel(pa_ref, pa_prev_ref, pa_next_ref, pb_ref, pb_prev_ref, pb_next_ref,
                 wa_ref, ba_ref, lg_ref, lb_ref, wb_ref, ua_ref, ub_ref, bufa, bufb, *, t):
    si = pl.program_id(1)
    ns = pl.num_programs(1)
    has_prev = (si > 0).astype(F32)
    has_next = (si < ns - 1).astype(F32)

    def glu(ref):
        p = ref[...]
        return p[:, :W_MIX] * jax.nn.sigmoid(p[:, W_MIX:])

    def gated(ref):
        p = ref[...]
        return p[:, W_MIX:2 * W_MIX] * p[:, 2 * W_MIX:]

    bufa[0:HALO, :] = glu(pa_prev_ref) * has_prev
    bufa[HALO:HALO + t, :] = glu(pa_ref)
    bufa[HALO + t:2 * HALO + t, :] = glu(pa_next_ref) * has_next
    bufb[0:HALO, :] = gated(pb_prev_ref) * has_prev
    bufb[HALO:HALO + t, :] = gated(pb_ref)
    bufb[HALO + t:2 * HALO + t, :] = gated(pb_next_ref) * has_next

    first = HALO - CONV_A_WIDTH // 2
    acc = jnp.zeros((t, W_MIX), F32)
    for b in range(SUBLANES):
        z = None
        for a in range((2 * HALO) // SUBLANES):
            k = SUBLANES * a + b - first
            if 0 <= k < CONV_A_WIDTH:
                term = bufa[SUBLANES * a:SUBLANES * a + t + SUBLANES, :] * wa_ref[k:k + 1, :]
                z = term if z is None else z + term
        acc = acc + z[b:b + t, :]
    acc = acc + ba_ref[...]
    mu = jnp.mean(acc, axis=-1, keepdims=True)
    xc = acc - mu
    var = jnp.mean(xc * xc, axis=-1, keepdims=True)
    y = xc * lax.rsqrt(var + EPS) * lg_ref[...] + lb_ref[...]
    ua_ref[...] = (y * jax.nn.sigmoid(y)).astype(BF16)

    accb = jnp.zeros((t, W_MIX), F32)
    for k in range(CONV_B_WIDTH):
        start = HALO - CONV_B_WIDTH // 2 + k
        accb = accb + bufb[start:start + t, :] * wb_ref[k:k + 1, :]
    ub_ref[...] = (pb_ref[:, :W_MIX] * accb).astype(BF16)


def _conv_mixers(pa, pb, wa, ba, lg, lb, wb, t):
    nb, s, _ = pa.shape
    hb = t // HALO
    last = s // HALO - 1

    def main(b, i):
        return (b, i, 0)

    def prev(b, i):
        return (b, jnp.maximum(i * hb - 1, 0), 0)

    def nxt(b, i):
        return (b, jnp.minimum((i + 1) * hb, last), 0)

    const = lambda b, i: (0, 0)
    return pl.pallas_call(
        functools.partial(_conv_kernel, t=t),
        grid=(nb, s // t),
        in_specs=[pl.BlockSpec((None, t, SZ_A), main),
                  pl.BlockSpec((None, HALO, SZ_A), prev),
                  pl.BlockSpec((None, HALO, SZ_A), nxt),
                  pl.BlockSpec((None, t, SZ_B), main),
                  pl.BlockSpec((None, HALO, SZ_B), prev),
                  pl.BlockSpec((None, HALO, SZ_B), nxt),
                  pl.BlockSpec((CONV_A_WIDTH, W_MIX), const),
                  pl.BlockSpec((1, W_MIX), const),
                  pl.BlockSpec((1, W_MIX), const),
                  pl.BlockSpec((1, W_MIX), const),
                  pl.BlockSpec((CONV_B_WIDTH, W_MIX), const)],
        out_specs=[pl.BlockSpec((None, t, W_MIX), main),
                   pl.BlockSpec((None, t, W_MIX), main)],
        out_shape=[jax.ShapeDtypeStruct((nb, s, W_MIX), BF16)] * 2,
        scratch_shapes=[pltpu.VMEM((t + 2 * HALO, W_MIX), F32),
                        pltpu.VMEM((t + 2 * HALO, W_MIX), F32)],
        compiler_params=_cparams(("parallel", "parallel")),
        name="conv_mixers",
    )(pa, pa, pa, pb, pb, pb, wa, ba.reshape(1, W_MIX), lg.reshape(1, W_MIX),
      lb.reshape(1, W_MIX), wb)


def _head_sumsq(x, ones_bd):
    x2 = x * x
    hi = x2.astype(BF16)
    lo = (x2 - hi.astype(F32)).astype(BF16)
    return (jnp.dot(hi, ones_bd, preferred_element_type=F32)
            + jnp.dot(lo, ones_bd, preferred_element_type=F32))


def _rope(x, cos, sin_signed, half):
    w = x.shape[-1]
    lane = lax.broadcasted_iota(jnp.int32, x.shape, 1)
    fwd = pltpu.roll(x, w - half, 1)
    bwd = pltpu.roll(x, half, 1)
    rot = jnp.where(lane % (2 * half) < half, fwd, bwd)
    return x * cos + rot * sin_signed


def _prep_rows(pc, pd, gains, cosc, sinc, cosd, sind, ones_bd, c_refs, qd_ref, kd_ref, vd_ref,
               qkv_scr):
    scale = HEAD_DIM ** -0.5

    def normed(x, g, ones):
        ss = _head_sumsq(x, ones)
        return x * lax.rsqrt(ss * (1.0 / HEAD_DIM) + EPS) * g

    qc = _rope(normed(pc[:, 0:256], gains[0:1, :], ones_bd), cosc, sinc, HEAD_DIM // 2) * (scale * LOG2E)
    kc = _rope(normed(pc[:, 256:512], gains[1:2, :], ones_bd), cosc, sinc, HEAD_DIM // 2)
    qkv = (qc, kc, pc[:, 512:768])
    ncb = SZ_C // LANES
    for c in range(ncb):
        part = qkv[c * LANES // W_MIX]
        lo = c * LANES % W_MIX
        qkv_scr[c] = part[:, lo:lo + LANES]
    ts = pc.shape[0]
    for ref, (_, dil) in zip(c_refs, DIL_PATTERNS):
        for r in range(dil):
            for c in range(ncb):
                ref[r, :, c * LANES:(c + 1) * LANES] = (
                    qkv_scr[c, pl.ds(r, ts // dil, stride=dil), :].astype(BF16))

    qd = _rope(normed(pd[:, 0:256], gains[2:3, :], ones_bd), cosd, sind, HEAD_DIM // 4) * (scale * LOG2E)
    kd = _rope(normed(pd[:, 256:384], gains[3:4, 0:128], ones_bd[0:128, 0:128]),
               cosd[:, 0:128], sind[:, 0:128], HEAD_DIM // 4)
    vd = pd[:, 384:512]
    group = N_HEADS // N_KV_HEADS
    for h in range(N_HEADS):
        qd_ref[h // group, h % group] = qd[:, h * HEAD_DIM:(h + 1) * HEAD_DIM].astype(BF16)
    lane = lax.broadcasted_iota(jnp.int32, vd.shape, 1)
    kd_t = kd.T.astype(BF16)
    for h in range(N_KV_HEADS):
        kd_ref[h] = kd_t[h * HEAD_DIM:(h + 1) * HEAD_DIM, :]
        vh = vd if h == 0 else pltpu.roll(vd, LANES - h * HEAD_DIM, 1)
        vd_ref[h] = jnp.where(lane < HEAD_DIM, vh, jnp.where(lane == HEAD_DIM, 1.0, 0.0)).astype(BF16)


def _rope_tables(s):
    def angles(pos, dim):
        inv = ROPE_THETA ** (-jnp.arange(0, dim, 2, dtype=F32) / dim)
        return pos.astype(F32)[:, None] * inv[None, :]

    pos = jnp.arange(s)
    ang = angles(pos, HEAD_DIM)
    cos_c = jnp.tile(jnp.concatenate([jnp.cos(ang), jnp.cos(ang)], -1), (1, N_HEADS))
    sin_c = jnp.tile(jnp.concatenate([-jnp.sin(ang), jnp.sin(ang)], -1), (1, N_HEADS))
    row = pos // GRID_W
    col = pos % GRID_W
    ar = angles(row, HEAD_DIM // 2)
    ac = angles(col, HEAD_DIM // 2)
    cos_d = jnp.tile(jnp.concatenate([jnp.cos(ar), jnp.cos(ar), jnp.cos(ac), jnp.cos(ac)], -1),
                     (1, N_HEADS))
    sin_d = jnp.tile(jnp.concatenate([-jnp.sin(ar), jnp.sin(ar), -jnp.sin(ac), jnp.sin(ac)], -1),
                     (1, N_HEADS))
    return cos_c, sin_c, cos_d, sin_d


def _dilated_kernel(q_ref, k_ref, v_ref, o_ref, lse_ref, *, seq_sub, half, lq):
    span = Q_BLOCK + 2 * half
    qi = pl.program_id(2)
    head_of_row = lax.broadcasted_iota(jnp.int32, (W_MIX, Q_BLOCK), 0) // HEAD_DIM
    for i in range(lq // Q_BLOCK):
        q0 = i * Q_BLOCK
        qabs = qi * lq + q0
        ks = pl.multiple_of(jnp.clip(qabs - half, 0, seq_sub - span), half)
        qt = q_ref[q0:q0 + Q_BLOCK, :].astype(F32).T
        qbd = jnp.concatenate([jnp.where(head_of_row == h, qt, 0.0) for h in range(N_HEADS)],
                              axis=1).astype(BF16)
        k = k_ref[pl.ds(ks, span), :]
        vt = v_ref[pl.ds(ks, span), :].astype(F32).T.astype(BF16)
        st = jnp.dot(k, qbd, preferred_element_type=F32)
        kpos = ks + lax.broadcasted_iota(jnp.int32, (span, Q_BLOCK), 0)
        qpos = qabs + lax.broadcasted_iota(jnp.int32, (span, Q_BLOCK), 1)
        mask = jnp.abs(kpos - qpos) <= half
        pts, ls, lses = [], [], []
        for h in range(N_HEADS):
            sc = jnp.where(mask, st[:, h * Q_BLOCK:(h + 1) * Q_BLOCK], NEG_INF)
            m = jnp.max(sc, axis=0, keepdims=True)
            p = jnp.exp2(sc - m)
            l = jnp.sum(p, axis=0, keepdims=True)
            pts.append(p.astype(BF16))
            ls.append(l)
            lses.append(m + jnp.log2(l))
        ot_all = jnp.dot(vt, jnp.concatenate(pts, axis=1), preferred_element_type=F32)
        ot = jnp.concatenate(
            [ot_all[h * HEAD_DIM:(h + 1) * HEAD_DIM, h * Q_BLOCK:(h + 1) * Q_BLOCK] / ls[h]
             for h in range(N_HEADS)], axis=0)
        lset = jnp.concatenate([jnp.broadcast_to(x, (HEAD_DIM, Q_BLOCK)) for x in lses], axis=0)
        o_ref[q0:q0 + Q_BLOCK, :] = ot.T
        lse_ref[q0:q0 + Q_BLOCK, :] = lset.T


def _dilated_pattern(qkv, win, dil):
    nb, _, seq_sub, _ = qkv.shape
    half = win // (2 * dil)
    lq = min(seq_sub, 1024)
    qspec = pl.BlockSpec((None, None, lq, W_MIX), lambda b, r, i: (b, r, i, 0))
    kspec = pl.BlockSpec((None, None, seq_sub, W_MIX), lambda b, r, i: (b, r, 0, 1))
    vspec = pl.BlockSpec((None, None, seq_sub, W_MIX), lambda b, r, i: (b, r, 0, 2))
    return pl.pallas_call(
        functools.partial(_dilated_kernel, seq_sub=seq_sub, half=half, lq=lq),
        grid=(nb, dil, seq_sub // lq),
        in_specs=[qspec, kspec, vspec],
        out_specs=[qspec, qspec],
        out_shape=[jax.ShapeDtypeStruct((nb, dil, seq_sub, W_MIX), F32)] * 2,
        compiler_params=_cparams(("parallel", "parallel", "arbitrary")),
        name=f"dilated_w{win}_d{dil}",
    )(qkv, qkv, qkv)


def _mixture_kernel(*refs, ts):
    n = len(DIL_PATTERNS)
    o_refs, l_refs = refs[0:2 * n:2], refs[1:2 * n:2]
    uc_ref = refs[2 * n]
    scr = refs[2 * n + 1:]

    def natural(ref, dil, buf):
        if dil == 1:
            return ref[0]
        ncb = W_MIX // LANES
        for r in range(dil):
            for c in range(ncb):
                buf[c, pl.ds(r, ts // dil, stride=dil), :] = ref[r, :, c * LANES:(c + 1) * LANES]
        return jnp.concatenate([buf[c] for c in range(ncb)], axis=-1)

    os_, ls_ = [], []
    for j, (_, dil) in enumerate(DIL_PATTERNS):
        os_.append(natural(o_refs[j], dil, scr[2 * j]))
        ls_.append(natural(l_refs[j], dil, scr[2 * j + 1]))
    mx = functools.reduce(jnp.maximum, ls_)
    ws = [jnp.exp2(l - mx) for l in ls_]
    num = sum(w * o for w, o in zip(ws, os_))
    uc_ref[...] = (num / sum(ws)).astype(uc_ref.dtype)


def _dilated_mixture(c_layouts, ts):
    nb, _, s, _ = c_layouts[0].shape
    ins, in_specs = [], []
    for qkv, (win, dil) in zip(c_layouts, DIL_PATTERNS):
        o, lse = _dilated_pattern(qkv, win, dil)
        spec = pl.BlockSpec((None, dil, ts // dil, W_MIX), lambda b, i: (b, 0, i, 0))
        ins += [o, lse]
        in_specs += [spec, spec]
    return pl.pallas_call(
        functools.partial(_mixture_kernel, ts=ts),
        grid=(nb, s // ts),
        in_specs=in_specs,
        out_specs=pl.BlockSpec((None, ts, W_MIX), lambda b, i: (b, i, 0)),
        out_shape=jax.ShapeDtypeStruct((nb, s, W_MIX), BF16),
        scratch_shapes=[pltpu.VMEM((W_MIX // LANES, ts, LANES), F32)] * (2 * len(DIL_PATTERNS)),
        compiler_params=_cparams(("parallel", "parallel")),
        name="dilated_mixture",
    )(*ins)


def _gqa_kernel(q_ref, k_ref, v_ref, o_ref, *, tq, tk, s):
    group = N_HEADS // N_KV_HEADS
    rows = group * tq
    qs = [q_ref[g] for g in range(N_KV_HEADS)]

    def body(c, carry):
        k0 = pl.multiple_of(c * tk, tk)
        new = []
        for g in range(N_KV_HEADS):
            kt = k_ref[g, :, pl.ds(k0, tk)]
            v = v_ref[g, pl.ds(k0, tk), :]
            m, acc = carry[g]
            sc = jnp.dot(qs[g], kt, preferred_element_type=F32)
            m_new = jnp.maximum(m, jnp.max(sc, axis=-1, keepdims=True))
            alpha = jnp.exp2(m - m_new)
            p = jnp.exp2(sc - m_new).astype(BF16)
            acc = alpha * acc + jnp.dot(p, v, preferred_element_type=F32)
            new.append((m_new, acc))
        return tuple(new)

    init = tuple((jnp.full((rows, 1), -jnp.inf, F32), jnp.zeros((rows, LANES), F32))
                 for _ in range(N_KV_HEADS))
    fin = lax.fori_loop(0, s // tk, body, init)
    outs = []
    for (_, acc) in fin:
        o = acc[:, :HEAD_DIM] / acc[:, HEAD_DIM:HEAD_DIM + 1]
        outs += [o[j * tq:(j + 1) * tq, :] for j in range(group)]
    o_ref[...] = jnp.concatenate(outs, axis=-1).astype(o_ref.dtype)


def _gqa(qd, kd, vd, tq, tk):
    nb, _, nq, group, _, _ = qd.shape
    s = nq * tq
    qd = qd.reshape(nb, N_KV_HEADS, nq, group * tq, HEAD_DIM)
    return pl.pallas_call(
        functools.partial(_gqa_kernel, tq=tq, tk=tk, s=s),
        grid=(nb, nq),
        in_specs=[pl.BlockSpec((None, N_KV_HEADS, None, group * tq, HEAD_DIM),
                               lambda b, i: (b, 0, i, 0, 0)),
                  pl.BlockSpec((None, N_KV_HEADS, HEAD_DIM, s), lambda b, i: (b, 0, 0, 0)),
                  pl.BlockSpec((None, N_KV_HEADS, s, LANES), lambda b, i: (b, 0, 0, 0))],
        out_specs=pl.BlockSpec((None, tq, W_MIX), lambda b, i: (b, i, 0)),
        out_shape=jax.ShapeDtypeStruct((nb, s, W_MIX), BF16),
        compiler_params=_cparams(("parallel", "arbitrary")),
        name="gqa",
    )(qd, kd, vd)


def _outproj_kernel(*refs, starts):
    ns = len(starts)
    ua_ref, ub_ref, uc_ref, ud_ref = refs[:4]
    w_ref, g_ref, wr_ref, br_ref, x1_ref, xe_ref = refs[4 + ns:]
    acc = _pick_source(refs[4:4 + ns], starts, pl.program_id(0))
    for j, ref in enumerate((ua_ref, ub_ref, uc_ref, ud_ref)):
        acc = acc + jnp.dot(ref[...], w_ref[j * W_MIX:(j + 1) * W_MIX, :],
                            preferred_element_type=F32)
    x1_ref[...] = acc
    ms = jnp.mean(acc * acc, axis=-1, keepdims=True)
    xn = acc * lax.rsqrt(ms + EPS) * g_ref[...]
    logits = jnp.dot(xn.astype(BF16), wr_ref[...], preferred_element_type=F32) + br_ref[...]
    xe_ref[:, :D_MODEL] = xn
    xe_ref[:, D_MODEL:] = _route(logits)


def _route(logits):
    lane = lax.broadcasted_iota(jnp.int32, logits.shape, 1)
    big = jnp.int32(LANES)
    ninf = -jnp.inf

    def argmax(vals):
        mx = jnp.max(vals, axis=-1, keepdims=True)
        idx = jnp.min(jnp.where(vals == mx, lane, big), axis=-1, keepdims=True)
        return mx, idx

    glog = jnp.where(lane < N_GROUPS, logits, ninf)
    gmax, gidx = argmax(glog)
    g_w = 1.0 / jnp.sum(jnp.exp(glog - gmax), axis=-1, keepdims=True)
    lo = N_GROUPS + gidx * EXPERTS_PER_GROUP
    el = jnp.where((lane >= lo) & (lane < lo + EXPERTS_PER_GROUP), logits, ninf)
    t1, i1 = argmax(el)
    el2 = jnp.where(lane == i1, ninf, el)
    t2, i2 = argmax(el2)
    e2 = jnp.exp(t2 - t1)
    w1 = 1.0 / (1.0 + e2)
    w2 = e2 * w1
    gates = jnp.where(lane == i1, w1 * g_w, jnp.where(lane == i2, w2 * g_w, 0.0))
    a = jnp.minimum(i1, i2) - lo
    b = jnp.maximum(i1, i2) - lo
    pair = lax.shift_right_logical(a * (2 * EXPERTS_PER_GROUP - 1 - a), 1) + (b - a - 1)
    cls = gidx * PAIRS_PER_GROUP + pair
    return jnp.where(lane == 0, cls.astype(F32), gates)


def _outproj(ua, ub, uc, ud, xsrcs, w_bf16, g2, wr, br, tm):
    xspecs, starts, ntiles = _row_source_specs(xsrcs, tm)
    n = ntiles * tm
    row = lambda i: (i, 0)
    const = lambda i: (0, 0)
    return pl.pallas_call(
        functools.partial(_outproj_kernel, starts=starts),
        grid=(ntiles,),
        in_specs=[pl.BlockSpec((tm, W_MIX), row)] * 4 + xspecs
        + [pl.BlockSpec((D_MODEL, D_MODEL), const),
           pl.BlockSpec((1, D_MODEL), const),
           pl.BlockSpec((D_MODEL, LANES), const),
           pl.BlockSpec((1, LANES), const)],
        out_specs=[pl.BlockSpec((tm, D_MODEL), row),
                   pl.BlockSpec((tm, D_ROW), row)],
        out_shape=[jax.ShapeDtypeStruct((n, D_MODEL), F32),
                   jax.ShapeDtypeStruct((n, D_ROW), F32)],
        compiler_params=_cparams(("parallel",)),
        name="outproj_router",
    )(ua, ub, uc, ud, *xsrcs, w_bf16, g2.reshape(1, D_MODEL), wr, br)


def _class_experts():
    e1, e2 = [], []
    for g in range(N_GROUPS):
        for a in range(EXPERTS_PER_GROUP):
            for b in range(a + 1, EXPERTS_PER_GROUP):
                e1.append(g * EXPERTS_PER_GROUP + a)
                e2.append(g * EXPERTS_PER_GROUP + b)
    return jnp.array(e1, jnp.int32), jnp.array(e2, jnp.int32)


def _moe_plan(cls, tm):
    n = cls.shape[0]
    nt = n // tm + N_CLASSES
    classes = jnp.arange(N_CLASSES, dtype=jnp.int32)
    counts = jnp.sum((cls[:, None] == classes[None, :]).astype(jnp.int32), axis=0)
    _, order = lax.sort((cls, jnp.arange(n, dtype=jnp.int32)), num_keys=1, is_stable=True)
    cstart = jnp.cumsum(counts) - counts
    ctiles = (counts + tm - 1) // tm
    tstart = jnp.cumsum(ctiles) - ctiles
    t = jnp.arange(nt, dtype=jnp.int32)
    tclass = jnp.clip(jnp.sum((t[:, None] >= tstart[None, :]).astype(jnp.int32), axis=1) - 1,
                      0, N_CLASSES - 1)
    p = jnp.arange(nt * tm, dtype=jnp.int32)
    room = N_CLASSES * tm
    ext = jnp.concatenate([jnp.zeros((room,), jnp.int32), order, jnp.zeros((room,), jnp.int32)])
    src = jnp.full((nt * tm,), -1, jnp.int32)
    for c in range(N_CLASSES):
        lo = tstart[c] * tm
        shifted = lax.dynamic_slice(ext, (room - (lo - cstart[c]),), (nt * tm,))
        src = jnp.where((p >= lo) & (p < lo + counts[c]), shifted, src)
    trash = n + p % (2 * tm)
    src_g = jnp.where(src < 0, 0, src).reshape(nt, 1, tm)
    src_s = jnp.where(src < 0, trash, src).reshape(nt, 1, tm)
    e1, e2 = _class_experts()
    return e1[tclass], e2[tclass], src_g, src_s


def _moe_kernel(e1_ref, e2_ref, srcg_cur_ref, srcg_nxt_ref, srcs_ref, xe_hbm,
                wg1_ref, wu1_ref, wd1_ref, wg2_ref, wu2_ref, wd2_ref,
                y_hbm, xg, ys, gsem, ssem, *, tm, n):
    t = pl.program_id(0)
    nt = pl.num_programs(0)
    slot = t % 2

    def row_gather(src_ref, dst_slot):
        for r in range(tm):
            tok = src_ref[0, r]
            pltpu.make_async_copy(xe_hbm.at[tok], xg.at[dst_slot, r],
                                  gsem.at[dst_slot]).start(priority=r % 2)

    def tile_gather_wait(s_):
        pltpu.make_async_copy(xe_hbm.at[pl.ds(0, tm)], xg.at[s_], gsem.at[s_]).wait()

    def tile_scatter_wait(s_):
        pltpu.make_async_copy(ys.at[s_], y_hbm.at[pl.ds(0, tm)], ssem.at[s_]).wait()

    @pl.when(t == 0)
    def _():
        row_gather(srcg_cur_ref, 0)
        ys[1] = jnp.zeros((tm, D_MODEL), F32)
        for half in range(2):
            cp = pltpu.make_async_copy(ys.at[1], y_hbm.at[pl.ds(n + half * tm, tm)], ssem.at[1])
            cp.start()
            cp.wait()

    tile_gather_wait(slot)

    @pl.when(t >= 2)
    def _():
        tile_scatter_wait(slot)

    row_gather(srcg_nxt_ref, 1 - slot)

    xe = xg[slot]
    x = xe[:, :D_MODEL].astype(BF16)
    gates = xe[:, D_MODEL:]
    lane = lax.broadcasted_iota(jnp.int32, gates.shape, 1)
    y = jnp.zeros((tm, D_MODEL), F32)
    for e_ref, wg_ref, wu_ref, wd_ref in ((e1_ref, wg1_ref, wu1_ref, wd1_ref),
                                          (e2_ref, wg2_ref, wu2_ref, wd2_ref)):
        ge = jnp.sum(jnp.where(lane == N_GROUPS + e_ref[t], gates, 0.0), axis=-1, keepdims=True)
        a = jnp.dot(x, wg_ref[...], preferred_element_type=F32)
        u = jnp.dot(x, wu_ref[...], preferred_element_type=F32)
        hdn = (a * jax.nn.sigmoid(a) * u).astype(BF16)
        y = y + ge * jnp.dot(hdn, wd_ref[...], preferred_element_type=F32)
    ys[slot] = y

    for r in range(tm):
        tok = srcs_ref[0, r]
        pltpu.make_async_copy(ys.at[slot, r], y_hbm.at[tok], ssem.at[slot]).start(priority=r % 2)

    @pl.when(t == nt - 1)
    def _():
        tile_gather_wait(1 - slot)
        tile_scatter_wait(slot)

        @pl.when(nt >= 2)
        def _():
            tile_scatter_wait(1 - slot)


def _moe(xe, wg, wu, wd, tm):
    n = xe.shape[0]
    cls = xe[:, D_MODEL].astype(jnp.int32)
    e1, e2, src_g, src_s = _moe_plan(cls, tm)
    nt = e1.shape[0]
    smem = lambda imap: pl.BlockSpec((None, 1, tm), imap, memory_space=pltpu.SMEM)
    w1 = lambda shp: pl.BlockSpec((None,) + shp, lambda t, e1, e2: (e1[t], 0, 0))
    w2 = lambda shp: pl.BlockSpec((None,) + shp, lambda t, e1, e2: (e2[t], 0, 0))
    up, down = (D_MODEL, D_EXPERT), (D_EXPERT, D_MODEL)
    grid_spec = pltpu.PrefetchScalarGridSpec(
        num_scalar_prefetch=2,
        grid=(nt,),
        in_specs=[smem(lambda t, e1, e2: (t, 0, 0)),
                  smem(lambda t, e1, e2: (jnp.minimum(t + 1, nt - 1), 0, 0)),
                  smem(lambda t, e1, e2: (t, 0, 0)),
                  pl.BlockSpec(memory_space=pl.ANY),
                  w1(up), w1(up), w1(down), w2(up), w2(up), w2(down)],
        out_specs=pl.BlockSpec(memory_space=pl.ANY),
        scratch_shapes=[pltpu.VMEM((2, tm, D_ROW), F32),
                        pltpu.VMEM((2, tm, D_MODEL), F32),
                        pltpu.SemaphoreType.DMA((2,)),
                        pltpu.SemaphoreType.DMA((2,))])
    return pl.pallas_call(
        functools.partial(_moe_kernel, tm=tm, n=n),
        grid_spec=grid_spec,
        out_shape=jax.ShapeDtypeStruct((n + 2 * tm, D_MODEL), F32),
        compiler_params=_cparams(("arbitrary",)),
        name="moe",
    )(e1, e2, src_g, src_g, src_s, xe, wg, wu, wd, wg, wu, wd)


def _residual_kernel(x1_ref, y_ref, *o_refs, starts):
    i = pl.program_id(0)
    ends = starts[1:] + (pl.num_programs(0),)
    for o_ref, end in zip(o_refs, ends):
        @pl.when(i < end)
        def _():
            o_ref[...] = x1_ref[...] + y_ref[...]


def _residual(x1, y, tm, row_splits):
    row = lambda i: (i, 0)
    outs = [jax.ShapeDtypeStruct((r, D_MODEL), F32) for r in row_splits]
    ospecs, starts, ntiles = _row_source_specs(outs, tm)
    return pl.pallas_call(
        functools.partial(_residual_kernel, starts=starts),
        grid=(ntiles,),
        in_specs=[pl.BlockSpec((tm, D_MODEL), row), pl.BlockSpec((tm, D_MODEL), row)],
        out_specs=ospecs,
        out_shape=outs,
        compiler_params=_cparams(("arbitrary",)),
        name="moe_residual",
    )(x1, y)


def _tiles(nb, s):
    return dict(tm=min(512, s), tconv=min(512, s),
                tq=min(512, s), tk=min(2048, s), tmoe=min(256, s))


def _layer(xsrcs, nb, s, p, l, tables, tl, row_splits):
    n = nb * s
    gains = jnp.stack([jnp.tile(p[k][l], N_HEADS) for k in ("qn_c", "kn_c", "qn_d", "kn_d")])
    pa, pb, c1, c4, c16, qd, kd, vd = _inproj(xsrcs, p["norm1_g"][l], p["w_in"][l], gains, tables,
                                              nb, s, tl["tq"])
    pa = pa.reshape(nb, s, SZ_A)
    pb = pb.reshape(nb, s, SZ_B)
    ua, ub = _conv_mixers(pa, pb, p["conv_a_w"][l], p["conv_a_b"][l], p["ln_a_g"][l],
                          p["ln_a_b"][l], p["conv_b_w"][l], tl["tconv"])
    uc = _dilated_mixture((c1, c4, c16), tl["tm"])
    ud = _gqa(qd, kd, vd, tl["tq"], tl["tk"])
    x1, xe = _outproj(ua.reshape(n, W_MIX), ub.reshape(n, W_MIX), uc.reshape(n, W_MIX),
                      ud.reshape(n, W_MIX), xsrcs, p["w_out"][l], p["norm2_g"][l],
                      p["w_route"][l], p["b_route"][l], tl["tm"])
    y = _moe(xe, p["w_gate"][l], p["w_up"][l], p["w_down"][l], tl["tmoe"])
    return _residual(x1, y, tl["tm"], row_splits)


def _trunk(xs, p):
    s = xs[0].shape[1]
    assert all(x.shape[1:] == (s, D_MODEL) for x in xs)
    nb = sum(x.shape[0] for x in xs)
    tables = _rope_tables(s)
    tl = _tiles(nb, s)
    depth = p["w_in"].shape[0]
    srcs = [x.reshape(-1, D_MODEL) for x in xs]
    for l in range(depth):
        splits = [x.shape[0] * s for x in xs] if l == depth - 1 else [nb * s]
        srcs = _layer(srcs, nb, s, p, l, tables, tl, splits)
    return tuple(y.reshape(x.shape) for y, x in zip(srcs, xs))


def _pack_params(norm1_g, w_in, conv_a_w, conv_a_b, ln_a_g, ln_a_b, conv_b_w, qn_c, kn_c, qn_d, kn_d,
                 w_out, norm2_g, w_group, b_group, w_router, b_router, w_gate, w_up, w_down):
    depth = w_in.shape[0]
    pad = LANES - N_GROUPS - N_EXPERTS
    w_route = jnp.concatenate([w_group, w_router, jnp.zeros((depth, D_MODEL, pad), F32)], axis=-1)
    b_route = jnp.concatenate([b_group, b_router, jnp.zeros((depth, pad), F32)], axis=-1)
    return dict(norm1_g=norm1_g, w_in=w_in.astype(BF16), conv_a_w=conv_a_w, conv_a_b=conv_a_b,
                ln_a_g=ln_a_g, ln_a_b=ln_a_b, conv_b_w=conv_b_w, qn_c=qn_c, kn_c=kn_c, qn_d=qn_d,
                kn_d=kn_d, w_out=w_out.astype(BF16), norm2_g=norm2_g,
                w_route=w_route.astype(BF16), b_route=b_route.reshape(depth, 1, LANES),
                w_gate=w_gate.astype(BF16), w_up=w_up.astype(BF16), w_down=w_down.astype(BF16))


def kernel(x_prompt, x_sample, norm1_g, w_in, conv_a_w, conv_a_b, ln_a_g, ln_a_b, conv_b_w, qn_c, kn_c, qn_d, kn_d, w_out, norm2_g, w_group, b_group, w_router, b_router, w_gate, w_up, w_down):
    p = _pack_params(norm1_g, w_in, conv_a_w, conv_a_b, ln_a_g, ln_a_b, conv_b_w, qn_c, kn_c, qn_d,
                     kn_d, w_out, norm2_g, w_group, b_group, w_router, b_router, w_gate, w_up, w_down)
    return _trunk((x_prompt, x_sample), p)
```

```python
import functools
import math

import jax
import jax.numpy as jnp
from jax import lax
from jax.experimental import pallas as pl
from jax.experimental.pallas import tpu as pltpu

D_MODEL = 1024
HEAD_DIM = 64
W_MIX = 256
N_HEADS = 4
N_KV_HEADS = 2
CONV_A_WIDTH = 31
CONV_B_WIDTH = 3
DIL_PATTERNS = ((128, 1), (512, 4), (2048, 16))
Q_BLOCK = 128
GRID_W = 64
ROPE_THETA = 10000.0
N_GROUPS = 4
EXPERTS_PER_GROUP = 4
N_EXPERTS = 16
PAIRS_PER_GROUP = EXPERTS_PER_GROUP * (EXPERTS_PER_GROUP - 1) // 2
N_CLASSES = N_GROUPS * PAIRS_PER_GROUP
D_EXPERT = 512
EPS = 1e-6
NEG_INF = -1e30
LOG2E = math.log2(math.e)

SZ_A, SZ_B, SZ_C, SZ_D = 512, 768, 768, 512
D_IN = SZ_A + SZ_B + SZ_C + SZ_D
LANES = 128
SUBLANES = 8
HALO = 16
VMEM_LIMIT = 56 * 1024 * 1024

F32 = jnp.float32
BF16 = jnp.bfloat16


def _cparams(sem):
    return pltpu.CompilerParams(dimension_semantics=sem, vmem_limit_bytes=VMEM_LIMIT)


def _tile_of_source(i, *_, start, count):
    return (jnp.clip(i - start, 0, count - 1), 0)


def _row_source_specs(srcs, tm):
    specs, starts, start = [], [], 0
    for a in srcs:
        count = a.shape[0] // tm
        specs.append(pl.BlockSpec((tm, a.shape[1]),
                                  functools.partial(_tile_of_source, start=start, count=count)))
        starts.append(start)
        start += count
    return specs, tuple(starts), start


def _pick_source(refs, starts, i):
    x = refs[0][...]
    for ref, start in zip(refs[1:], starts[1:]):
        x = jnp.where(i >= start, ref[...], x)
    return x


def _inproj_kernel(*refs, starts):
    ns = len(starts)
    (g_ref, w_ref, gains_ref, cosc_ref, sinc_ref, cosd_ref, sind_ref, ones_ref,
     pa_ref, pb_ref, c1_ref, c4_ref, c16_ref, qd_ref, kd_ref, vd_ref, qkv_scr) = refs[ns:]
    x = _pick_source(refs[:ns], starts, pl.program_id(0))
    ms = jnp.mean(x * x, axis=-1, keepdims=True)
    h = (x * lax.rsqrt(ms + EPS) * g_ref[...]).astype(BF16)

    def proj(lo, sz):
        return jnp.dot(h, w_ref[:, lo:lo + sz], preferred_element_type=F32)

    pa_ref[...] = proj(0, SZ_A)
    pb_ref[...] = proj(SZ_A, SZ_B)
    _prep_rows(proj(SZ_A + SZ_B, SZ_C), proj(SZ_A + SZ_B + SZ_C, SZ_D), gains_ref[...],
               cosc_ref[...], sinc_ref[...], cosd_ref[...], sind_ref[...], ones_ref[...],
               (c1_ref, c4_ref, c16_ref), qd_ref, kd_ref, vd_ref, qkv_scr)


def _inproj(xsrcs, g, w_bf16, gains, tables, nb, s, tm):
    xspecs, starts, ntiles = _row_source_specs(xsrcs, tm)
    n = ntiles * tm
    tps = s // tm
    cos_c, sin_c, cos_d, sin_d = tables
    head = lax.broadcasted_iota(jnp.int32, (W_MIX, W_MIX), 0) // HEAD_DIM
    ones_bd = (head == head.T).astype(BF16)
    group = N_HEADS // N_KV_HEADS
    const = lambda i: (0, 0)
    row = lambda i: (i, 0)
    tab = lambda i: (i % tps, 0)
    hm = lambda i: (i // tps, 0, i % tps, 0)
    return pl.pallas_call(
        functools.partial(_inproj_kernel, starts=starts),
        grid=(ntiles,),
        in_specs=xspecs + [pl.BlockSpec((1, D_MODEL), const),
                           pl.BlockSpec((D_MODEL, D_IN), const),
                           pl.BlockSpec((4, W_MIX), const),
                           pl.BlockSpec((tm, W_MIX), tab),
                           pl.BlockSpec((tm, W_MIX), tab),
                           pl.BlockSpec((tm, W_MIX), tab),
                           pl.BlockSpec((tm, W_MIX), tab),
                           pl.BlockSpec((W_MIX, W_MIX), const)],
        out_specs=[pl.BlockSpec((tm, SZ_A), row), pl.BlockSpec((tm, SZ_B), row)]
        + [pl.BlockSpec((None, dil, tm // dil, SZ_C), hm) for _, dil in DIL_PATTERNS]
        + [pl.BlockSpec((None, N_KV_HEADS, None, group, tm, HEAD_DIM),
                        lambda i: (i // tps, 0, i % tps, 0, 0, 0)),
           pl.BlockSpec((None, N_KV_HEADS, HEAD_DIM, tm), lambda i: (i // tps, 0, 0, i % tps)),
           pl.BlockSpec((None, N_KV_HEADS, tm, LANES), hm)],
        out_shape=[jax.ShapeDtypeStruct((n, SZ_A), F32), jax.ShapeDtypeStruct((n, SZ_B), F32)]
        + [jax.ShapeDtypeStruct((nb, dil, s // dil, SZ_C), BF16) for _, dil in DIL_PATTERNS]
        + [jax.ShapeDtypeStruct((nb, N_KV_HEADS, tps, group, tm, HEAD_DIM), BF16),
           jax.ShapeDtypeStruct((nb, N_KV_HEADS, HEAD_DIM, s), BF16),
           jax.ShapeDtypeStruct((nb, N_KV_HEADS, s, LANES), BF16)],
        scratch_shapes=[pltpu.VMEM((SZ_C // LANES, tm, LANES), F32)],
        compiler_params=_cparams(("parallel",)),
        name="inproj",
    )(*xsrcs, g.reshape(1, D_MODEL), w_bf16, gains, cos_c, sin_c, cos_d, sin_d, ones_bd)


def _conv_kernel(pa_ref, pa_prev_ref, pa_next_ref, pb_ref, pb_prev_ref, pb_next_ref,
                 wa_ref, ba_ref, lg_ref, lb_ref, wb_ref, ua_ref, ub_ref, bufa, bufb, *, t):
    si = pl.program_id(1)
    ns = pl.num_programs(1)
    has_prev = (si > 0).astype(F32)
    has_next = (si < ns - 1).astype(F32)

    def glu(ref):
        p = ref[...]
        return p[:, :W_MIX] * jax.nn.sigmoid(p[:, W_MIX:])

    def gated(ref):
        p = ref[...]
        return p[:, W_MIX:2 * W_MIX] * p[:, 2 * W_MIX:]

    bufa[0:HALO, :] = glu(pa_prev_ref) * has_prev
    bufa[HALO:HALO + t, :] = glu(pa_ref)
    bufa[HALO + t:2 * HALO + t, :] = glu(pa_next_ref) * has_next
    bufb[0:HALO, :] = gated(pb_prev_ref) * has_prev
    bufb[HALO:HALO + t, :] = gated(pb_ref)
    bufb[HALO + t:2 * HALO + t, :] = gated(pb_next_ref) * has_next

    first = HALO - CONV_A_WIDTH // 2
    acc = jnp.zeros((t, W_MIX), F32)
    for b in range(SUBLANES):
        z = None
        for a in range((2 * HALO) // SUBLANES):
            k = SUBLANES * a + b - first
            if 0 <= k < CONV_A_WIDTH:
                term = bufa[SUBLANES * a:SUBLANES * a + t + SUBLANES, :] * wa_ref[k:k + 1, :]
                z = term if z is None else z + term
        acc = acc + z[b:b + t, :]
    acc = acc + ba_ref[...]
    mu = jnp.mean(acc, axis=-1, keepdims=True)
    xc = acc - mu
    var = jnp.mean(xc * xc, axis=-1, keepdims=True)
    y = xc * lax.rsqrt(var + EPS) * lg_ref[...] + lb_ref[...]
    ua_ref[...] = (y * jax.nn.sigmoid(y)).astype(BF16)

    accb = jnp.zeros((t, W_MIX), F32)
    for k in range(CONV_B_WIDTH):
        start = HALO - CONV_B_WIDTH // 2 + k
        accb = accb + bufb[start:start + t, :] * wb_ref[k:k + 1, :]
    ub_ref[...] = (pb_ref[:, :W_MIX] * accb).astype(BF16)


def _conv_mixers(pa, pb, wa, ba, lg, lb, wb, t):
    nb, s, _ = pa.shape
    hb = t // HALO
    last = s // HALO - 1

    def main(b, i):
        return (b, i, 0)

    def prev(b, i):
        return (b, jnp.maximum(i * hb - 1, 0), 0)

    def nxt(b, i):
        return (b, jnp.minimum((i + 1) * hb, last), 0)

    const = lambda b, i: (0, 0)
    return pl.pallas_call(
        functools.partial(_conv_kernel, t=t),
        grid=(nb, s // t),
        in_specs=[pl.BlockSpec((None, t, SZ_A), main),
                  pl.BlockSpec((None, HALO, SZ_A), prev),
                  pl.BlockSpec((None, HALO, SZ_A), nxt),
                  pl.BlockSpec((None, t, SZ_B), main),
                  pl.BlockSpec((None, HALO, SZ_B), prev),
                  pl.BlockSpec((None, HALO, SZ_B), nxt),
                  pl.BlockSpec((CONV_A_WIDTH, W_MIX), const),
                  pl.BlockSpec((1, W_MIX), const),
                  pl.BlockSpec((1, W_MIX), const),
                  pl.BlockSpec((1, W_MIX), const),
                  pl.BlockSpec((CONV_B_WIDTH, W_MIX), const)],
        out_specs=[pl.BlockSpec((None, t, W_MIX), main),
                   pl.BlockSpec((None, t, W_MIX), main)],
        out_shape=[jax.ShapeDtypeStruct((nb, s, W_MIX), BF16)] * 2,
        scratch_shapes=[pltpu.VMEM((t + 2 * HALO, W_MIX), F32),
                        pltpu.VMEM((t + 2 * HALO, W_MIX), F32)],
        compiler_params=_cparams(("parallel", "parallel")),
        name="conv_mixers",
    )(pa, pa, pa, pb, pb, pb, wa, ba.reshape(1, W_MIX), lg.reshape(1, W_MIX),
      lb.reshape(1, W_MIX), wb)


def _head_sumsq(x, ones_bd):
    x2 = x * x
    hi = x2.astype(BF16)
    lo = (x2 - hi.astype(F32)).astype(BF16)
    return (jnp.dot(hi, ones_bd, preferred_element_type=F32)
            + jnp.dot(lo, ones_bd, preferred_element_type=F32))


def _rope(x, cos, sin_signed, half):
    w = x.shape[-1]
    lane = lax.broadcasted_iota(jnp.int32, x.shape, 1)
    fwd = pltpu.roll(x, w - half, 1)
    bwd = pltpu.roll(x, half, 1)
    rot = jnp.where(lane % (2 * half) < half, fwd, bwd)
    return x * cos + rot * sin_signed


def _prep_rows(pc, pd, gains, cosc, sinc, cosd, sind, ones_bd, c_refs, qd_ref, kd_ref, vd_ref,
               qkv_scr):
    scale = HEAD_DIM ** -0.5

    def normed(x, g, ones):
        ss = _head_sumsq(x, ones)
        return x * lax.rsqrt(ss * (1.0 / HEAD_DIM) + EPS) * g

    qc = _rope(normed(pc[:, 0:256], gains[0:1, :], ones_bd), cosc, sinc, HEAD_DIM // 2) * (scale * LOG2E)
    kc = _rope(normed(pc[:, 256:512], gains[1:2, :], ones_bd), cosc, sinc, HEAD_DIM // 2)
    qkv = (qc, kc, pc[:, 512:768])
    ncb = SZ_C // LANES
    for c in range(ncb):
        part = qkv[c * LANES // W_MIX]
        lo = c * LANES % W_MIX
        qkv_scr[c] = part[:, lo:lo + LANES]
    ts = pc.shape[0]
    for ref, (_, dil) in zip(c_refs, DIL_PATTERNS):
        for r in range(dil):
            for c in range(ncb):
                ref[r, :, c * LANES:(c + 1) * LANES] = (
                    qkv_scr[c, pl.ds(r, ts // dil, stride=dil), :].astype(BF16))

    qd = _rope(normed(pd[:, 0:256], gains[2:3, :], ones_bd), cosd, sind, HEAD_DIM // 4) * (scale * LOG2E)
    kd = _rope(normed(pd[:, 256:384], gains[3:4, 0:128], ones_bd[0:128, 0:128]),
               cosd[:, 0:128], sind[:, 0:128], HEAD_DIM // 4)
    vd = pd[:, 384:512]
    group = N_HEADS // N_KV_HEADS
    for h in range(N_HEADS):
        qd_ref[h // group, h % group] = qd[:, h * HEAD_DIM:(h + 1) * HEAD_DIM].astype(BF16)
    lane = lax.broadcasted_iota(jnp.int32, vd.shape, 1)
    kd_t = kd.T.astype(BF16)
    for h in range(N_KV_HEADS):
        kd_ref[h] = kd_t[h * HEAD_DIM:(h + 1) * HEAD_DIM, :]
        vh = vd if h == 0 else pltpu.roll(vd, LANES - h * HEAD_DIM, 1)
        vd_ref[h] = jnp.where(lane < HEAD_DIM, vh, jnp.where(lane == HEAD_DIM, 1.0, 0.0)).astype(BF16)


def _rope_tables(s):
    def angles(pos, dim):
        inv = ROPE_THETA ** (-jnp.arange(0, dim, 2, dtype=F32) / dim)
        return pos.astype(F32)[:, None] * inv[None, :]

    pos = jnp.arange(s)
    ang = angles(pos, HEAD_DIM)
    cos_c = jnp.tile(jnp.concatenate([jnp.cos(ang), jnp.cos(ang)], -1), (1, N_HEADS))
    sin_c = jnp.tile(jnp.concatenate([-jnp.sin(ang), jnp.sin(ang)], -1), (1, N_HEADS))
    row = pos // GRID_W
    col = pos % GRID_W
    ar = angles(row, HEAD_DIM // 2)
    ac = angles(col, HEAD_DIM // 2)
    cos_d = jnp.tile(jnp.concatenate([jnp.cos(ar), jnp.cos(ar), jnp.cos(ac), jnp.cos(ac)], -1),
                     (1, N_HEADS))
    sin_d = jnp.tile(jnp.concatenate([-jnp.sin(ar), jnp.sin(ar), -jnp.sin(ac), jnp.sin(ac)], -1),
                     (1, N_HEADS))
    return cos_c, sin_c, cos_d, sin_d


def _dilated_kernel(q_ref, k_ref, v_ref, o_ref, lse_ref, *, seq_sub, half, lq):
    span = Q_BLOCK + 2 * half
    qi = pl.program_id(2)
    head_of_row = lax.broadcasted_iota(jnp.int32, (W_MIX, Q_BLOCK), 0) // HEAD_DIM
    for i in range(lq // Q_BLOCK):
        q0 = i * Q_BLOCK
        qabs = qi * lq + q0
        ks = pl.multiple_of(jnp.clip(qabs - half, 0, seq_sub - span), half)
        qt = q_ref[q0:q0 + Q_BLOCK, :].astype(F32).T
        qbd = jnp.concatenate([jnp.where(head_of_row == h, qt, 0.0) for h in range(N_HEADS)],
                              axis=1).astype(BF16)
        k = k_ref[pl.ds(ks, span), :]
        vt = v_ref[pl.ds(ks, span), :].astype(F32).T.astype(BF16)
        st = jnp.dot(k, qbd, preferred_element_type=F32)
        kpos = ks + lax.broadcasted_iota(jnp.int32, (span, Q_BLOCK), 0)
        qpos = qabs + lax.broadcasted_iota(jnp.int32, (span, Q_BLOCK), 1)
        mask = jnp.abs(kpos - qpos) <= half
        pts, ls, lses = [], [], []
        for h in range(N_HEADS):
            sc = jnp.where(mask, st[:, h * Q_BLOCK:(h + 1) * Q_BLOCK], NEG_INF)
            m = jnp.max(sc, axis=0, keepdims=True)
            p = jnp.exp2(sc - m)
            l = jnp.sum(p, axis=0, keepdims=True)
            pts.append(p.astype(BF16))
            ls.append(l)
            lses.append(m + jnp.log2(l))
        ot_all = jnp.dot(vt, jnp.concatenate(pts, axis=1), preferred_element_type=F32)
        ot = jnp.concatenate(
            [ot_all[h * HEAD_DIM:(h + 1) * HEAD_DIM, h * Q_BLOCK:(h + 1) * Q_BLOCK] / ls[h]
             for h in range(N_HEADS)], axis=0)
        lset = jnp.concatenate([jnp.broadcast_to(x, (HEAD_DIM, Q_BLOCK)) for x in lses], axis=0)
        o_ref[q0:q0 + Q_BLOCK, :] = ot.T
        lse_ref[q0:q0 + Q_BLOCK, :] = lset.T


def _dilated_pattern(qkv, win, dil):
    nb, _, seq_sub, _ = qkv.shape
    half = win // (2 * dil)
    lq = min(seq_sub, 1024)
    qspec = pl.BlockSpec((None, None, lq, W_MIX), lambda b, r, i: (b, r, i, 0))
    kspec = pl.BlockSpec((None, None, seq_sub, W_MIX), lambda b, r, i: (b, r, 0, 1))
    vspec = pl.BlockSpec((None, None, seq_sub, W_MIX), lambda b, r, i: (b, r, 0, 2))
    return pl.pallas_call(
        functools.partial(_dilated_kernel, seq_sub=seq_sub, half=half, lq=lq),
        grid=(nb, dil, seq_sub // lq),
        in_specs=[qspec, kspec, vspec],
        out_specs=[qspec, qspec],
        out_shape=[jax.ShapeDtypeStruct((nb, dil, seq_sub, W_MIX), F32)] * 2,
        compiler_params=_cparams(("parallel", "parallel", "arbitrary")),
        name=f"dilated_w{win}_d{dil}",
    )(qkv, qkv, qkv)


def _mixture_kernel(*refs, ts):
    n = len(DIL_PATTERNS)
    o_refs, l_refs = refs[0:2 * n:2], refs[1:2 * n:2]
    uc_ref = refs[2 * n]
    scr = refs[2 * n + 1:]

    def natural(ref, dil, buf):
        if dil == 1:
            return ref[0]
        ncb = W_MIX // LANES
        for r in range(dil):
            for c in range(ncb):
                buf[c, pl.ds(r, ts // dil, stride=dil), :] = ref[r, :, c * LANES:(c + 1) * LANES]
        return jnp.concatenate([buf[c] for c in range(ncb)], axis=-1)

    os_, ls_ = [], []
    for j, (_, dil) in enumerate(DIL_PATTERNS):
        os_.append(natural(o_refs[j], dil, scr[2 * j]))
        ls_.append(natural(l_refs[j], dil, scr[2 * j + 1]))
    mx = functools.reduce(jnp.maximum, ls_)
    ws = [jnp.exp2(l - mx) for l in ls_]
    num = sum(w * o for w, o in zip(ws, os_))
    uc_ref[...] = (num / sum(ws)).astype(uc_ref.dtype)


def _dilated_mixture(c_layouts, ts):
    nb, _, s, _ = c_layouts[0].shape
    ins, in_specs = [], []
    for qkv, (win, dil) in zip(c_layouts, DIL_PATTERNS):
        o, lse = _dilated_pattern(qkv, win, dil)
        spec = pl.BlockSpec((None, dil, ts // dil, W_MIX), lambda b, i: (b, 0, i, 0))
        ins += [o, lse]
        in_specs += [spec, spec]
    return pl.pallas_call(
        functools.partial(_mixture_kernel, ts=ts),
        grid=(nb, s // ts),
        in_specs=in_specs,
        out_specs=pl.BlockSpec((None, ts, W_MIX), lambda b, i: (b, i, 0)),
        out_shape=jax.ShapeDtypeStruct((nb, s, W_MIX), BF16),
        scratch_shapes=[pltpu.VMEM((W_MIX // LANES, ts, LANES), F32)] * (2 * len(DIL_PATTERNS)),
        compiler_params=_cparams(("parallel", "parallel")),
        name="dilated_mixture",
    )(*ins)


def _gqa_kernel(q_ref, k_ref, v_ref, o_ref, *, tq, tk, s):
    group = N_HEADS // N_KV_HEADS
    rows = group * tq
    qs = [q_ref[g] for g in range(N_KV_HEADS)]

    def body(c, carry):
        k0 = pl.multiple_of(c * tk, tk)
        new = []
        for g in range(N_KV_HEADS):
            kt = k_ref[g, :, pl.ds(k0, tk)]
            v = v_ref[g, pl.ds(k0, tk), :]
            m, acc = carry[g]
            sc = jnp.dot(qs[g], kt, preferred_element_type=F32)
            m_new = jnp.maximum(m, jnp.max(sc, axis=-1, keepdims=True))
            alpha = jnp.exp2(m - m_new)
            p = jnp.exp2(sc - m_new).astype(BF16)
            acc = alpha * acc + jnp.dot(p, v, preferred_element_type=F32)
            new.append((m_new, acc))
        return tuple(new)

    init = tuple((jnp.full((rows, 1), -jnp.inf, F32), jnp.zeros((rows, LANES), F32))
                 for _ in range(N_KV_HEADS))
    fin = lax.fori_loop(0, s // tk, body, init)
    outs = []
    for (_, acc) in fin:
        o = acc[:, :HEAD_DIM] / acc[:, HEAD_DIM:HEAD_DIM + 1]
        outs += [o[j * tq:(j + 1) * tq, :] for j in range(group)]
    o_ref[...] = jnp.concatenate(outs, axis=-1).astype(o_ref.dtype)


def _gqa(qd, kd, vd, tq, tk):
    nb, _, nq, group, _, _ = qd.shape
    s = nq * tq
    qd = qd.reshape(nb, N_KV_HEADS, nq, group * tq, HEAD_DIM)
    return pl.pallas_call(
        functools.partial(_gqa_kernel, tq=tq, tk=tk, s=s),
        grid=(nb, nq),
        in_specs=[pl.BlockSpec((None, N_KV_HEADS, None, group * tq, HEAD_DIM),
                               lambda b, i: (b, 0, i, 0, 0)),
                  pl.BlockSpec((None, N_KV_HEADS, HEAD_DIM, s), lambda b, i: (b, 0, 0, 0)),
                  pl.BlockSpec((None, N_KV_HEADS, s, LANES), lambda b, i: (b, 0, 0, 0))],
        out_specs=pl.BlockSpec((None, tq, W_MIX), lambda b, i: (b, i, 0)),
        out_shape=jax.ShapeDtypeStruct((nb, s, W_MIX), BF16),
        compiler_params=_cparams(("parallel", "arbitrary")),
        name="gqa",
    )(qd, kd, vd)


def _outproj_kernel(*refs, starts):
    ns = len(starts)
    ua_ref, ub_ref, uc_ref, ud_ref = refs[:4]
    w_ref, g_ref, wr_ref, br_ref, x1_ref, xn_ref, cls_ref = refs[4 + ns:]
    acc = _pick_source(refs[4:4 + ns], starts, pl.program_id(0))
    for j, ref in enumerate((ua_ref, ub_ref, uc_ref, ud_ref)):
        acc = acc + jnp.dot(ref[...], w_ref[j * W_MIX:(j + 1) * W_MIX, :],
                            preferred_element_type=F32)
    x1_ref[...] = acc
    ms = jnp.mean(acc * acc, axis=-1, keepdims=True)
    xn = acc * lax.rsqrt(ms + EPS) * g_ref[...]
    logits = jnp.dot(xn.astype(BF16), wr_ref[...], preferred_element_type=F32) + br_ref[...]
    _to_token_tiles(xn_ref, xn)
    cls_ref[...] = jnp.broadcast_to(_route_class(logits).astype(F32), cls_ref.shape)


def _to_token_tiles(ref, x):
    t = x.shape[0]
    for j in range(D_MODEL // LANES):
        ref[pl.ds(j, t, stride=SUBLANES), :] = x[:, j * LANES:(j + 1) * LANES]


def _from_token_tiles(ref, t):
    return jnp.concatenate([ref[pl.ds(j, t, stride=SUBLANES), :] for j in range(D_MODEL // LANES)],
                           axis=1)


def _route_class(logits):
    lane = lax.broadcasted_iota(jnp.int32, logits.shape, 1)
    big = jnp.int32(LANES)
    ninf = -jnp.inf

    def argmax(vals):
        mx = jnp.max(vals, axis=-1, keepdims=True)
        return jnp.min(jnp.where(vals == mx, lane, big), axis=-1, keepdims=True)

    gidx = argmax(jnp.where(lane < N_GROUPS, logits, ninf))
    lo = N_GROUPS + gidx * EXPERTS_PER_GROUP
    el = jnp.where((lane >= lo) & (lane < lo + EXPERTS_PER_GROUP), logits, ninf)
    i1 = argmax(el)
    i2 = argmax(jnp.where(lane == i1, ninf, el))
    a = jnp.minimum(i1, i2) - lo
    b = jnp.maximum(i1, i2) - lo
    pair = lax.shift_right_logical(a * (2 * EXPERTS_PER_GROUP - 1 - a), 1) + (b - a - 1)
    return gidx * PAIRS_PER_GROUP + pair


def _pair_gates(logits, e1, e2):
    lane = lax.broadcasted_iota(jnp.int32, logits.shape, 1)
    pick = lambda idx: jnp.sum(jnp.where(lane == idx, logits, 0.0), axis=-1, keepdims=True)
    glog = jnp.where(lane < N_GROUPS, logits, -jnp.inf)
    gmax = jnp.max(glog, axis=-1, keepdims=True)
    g_w = jnp.exp(pick(e1 // EXPERTS_PER_GROUP) - gmax) / jnp.sum(jnp.exp(glog - gmax), axis=-1,
                                                                    keepdims=True)
    l1, l2 = pick(N_GROUPS + e1), pick(N_GROUPS + e2)
    mx = jnp.maximum(l1, l2)
    a1, a2 = jnp.exp(l1 - mx), jnp.exp(l2 - mx)
    inv = g_w / (a1 + a2)
    return a1 * inv, a2 * inv


def _outproj(ua, ub, uc, ud, xsrcs, w_bf16, g2, wr, br, tm):
    xspecs, starts, ntiles = _row_source_specs(xsrcs, tm)
    n = ntiles * tm
    row = lambda i: (i, 0)
    const = lambda i: (0, 0)
    return pl.pallas_call(
        functools.partial(_outproj_kernel, starts=starts),
        grid=(ntiles,),
        in_specs=[pl.BlockSpec((tm, W_MIX), row)] * 4 + xspecs
        + [pl.BlockSpec((D_MODEL, D_MODEL), const),
           pl.BlockSpec((1, D_MODEL), const),
           pl.BlockSpec((D_MODEL, LANES), const),
           pl.BlockSpec((1, LANES), const)],
        out_specs=[pl.BlockSpec((tm, D_MODEL), row),
                   pl.BlockSpec((tm * SUBLANES, LANES), row),
                   pl.BlockSpec((tm, LANES), row)],
        out_shape=[jax.ShapeDtypeStruct((n, D_MODEL), F32),
                   jax.ShapeDtypeStruct((n * SUBLANES, LANES), F32),
                   jax.ShapeDtypeStruct((n, LANES), F32)],
        compiler_params=_cparams(("parallel",)),
        name="outproj_router",
    )(ua, ub, uc, ud, *xsrcs, w_bf16, g2.reshape(1, D_MODEL), wr, br)


def _class_experts():
    e1, e2 = [], []
    for g in range(N_GROUPS):
        for a in range(EXPERTS_PER_GROUP):
            for b in range(a + 1, EXPERTS_PER_GROUP):
                e1.append(g * EXPERTS_PER_GROUP + a)
                e2.append(g * EXPERTS_PER_GROUP + b)
    return jnp.array(e1, jnp.int32), jnp.array(e2, jnp.int32)


def _moe_plan(cls, tm):
    n = cls.shape[0]
    nt = n // tm + N_CLASSES
    classes = jnp.arange(N_CLASSES, dtype=jnp.int32)
    counts = jnp.sum((cls[:, None] == classes[None, :]).astype(jnp.int32), axis=0)
    _, order = lax.sort((cls, jnp.arange(n, dtype=jnp.int32)), num_keys=1, is_stable=True)
    cstart = jnp.cumsum(counts) - counts
    ctiles = (counts + tm - 1) // tm
    tstart = jnp.cumsum(ctiles) - ctiles
    t = jnp.arange(nt, dtype=jnp.int32)
    tclass = jnp.clip(jnp.sum((t[:, None] >= tstart[None, :]).astype(jnp.int32), axis=1) - 1,
                      0, N_CLASSES - 1)
    p = jnp.arange(nt * tm, dtype=jnp.int32)
    room = N_CLASSES * tm
    ext = jnp.concatenate([jnp.zeros((room,), jnp.int32), order, jnp.zeros((room,), jnp.int32)])
    src = jnp.full((nt * tm,), -1, jnp.int32)
    for c in range(N_CLASSES):
        lo = tstart[c] * tm
        shifted = lax.dynamic_slice(ext, (room - (lo - cstart[c]),), (nt * tm,))
        src = jnp.where((p >= lo) & (p < lo + counts[c]), shifted, src)
    trash = n + p % (2 * tm)
    src_g = jnp.where(src < 0, 0, src).reshape(nt, 1, tm)
    src_s = jnp.where(src < 0, trash, src).reshape(nt, 1, tm)
    e1, e2 = _class_experts()
    return e1[tclass], e2[tclass], src_g, src_s


def _moe_kernel(e1_ref, e2_ref, srcg_cur_ref, srcg_nxt_ref, srcs_ref, xn_hbm, wr_ref, br_ref,
                wg1_ref, wu1_ref, wd1_ref, wg2_ref, wu2_ref, wd2_ref,
                y_hbm, xg, ys, gsem, ssem, *, tm, n):
    t = pl.program_id(0)
    nt = pl.num_programs(0)
    slot = t % 2
    rows = tm * SUBLANES

    def token(ref, tok):
        start = tok * SUBLANES
        if not isinstance(tok, int):
            start = pl.multiple_of(start, SUBLANES)
        return ref.at[pl.ds(start, SUBLANES)]

    def gather(src_ref, dst_slot):
        for r in range(tm):
            pltpu.make_async_copy(token(xn_hbm, src_ref[0, r]), token(xg.at[dst_slot], r),
                                  gsem.at[dst_slot]).start(priority=r % 2)

    def tile_gather_wait(s_):
        pltpu.make_async_copy(xn_hbm.at[pl.ds(0, rows)], xg.at[s_], gsem.at[s_]).wait()

    def tile_scatter_wait(s_):
        pltpu.make_async_copy(ys.at[s_], y_hbm.at[pl.ds(0, rows)], ssem.at[s_]).wait()

    @pl.when(t == 0)
    def _():
        gather(srcg_cur_ref, 0)
        ys[1] = jnp.zeros((rows, LANES), F32)
        for half in range(2):
            cp = pltpu.make_async_copy(ys.at[1], y_hbm.at[pl.ds((n + half * tm) * SUBLANES, rows)],
                                       ssem.at[1])
            cp.start()
            cp.wait()

    tile_gather_wait(slot)

    @pl.when(t >= 2)
    def _():
        tile_scatter_wait(slot)

    gather(srcg_nxt_ref, 1 - slot)

    x = _from_token_tiles(xg.at[slot], tm).astype(BF16)
    logits = jnp.dot(x, wr_ref[...], preferred_element_type=F32) + br_ref[...]
    gates = _pair_gates(logits, e1_ref[t], e2_ref[t])
    y = jnp.zeros((tm, D_MODEL), F32)
    for ge, wg_ref, wu_ref, wd_ref in ((gates[0], wg1_ref, wu1_ref, wd1_ref),
                                       (gates[1], wg2_ref, wu2_ref, wd2_ref)):
        a = jnp.dot(x, wg_ref[...], preferred_element_type=F32)
        u = jnp.dot(x, wu_ref[...], preferred_element_type=F32)
        hdn = (a * jax.nn.sigmoid(a) * u).astype(BF16)
        y = y + ge * jnp.dot(hdn, wd_ref[...], preferred_element_type=F32)
    _to_token_tiles(ys.at[slot], y)

    for r in range(tm):
        pltpu.make_async_copy(token(ys.at[slot], r), token(y_hbm, srcs_ref[0, r]),
                              ssem.at[slot]).start(priority=r % 2)

    @pl.when(t == nt - 1)
    def _():
        tile_gather_wait(1 - slot)
        tile_scatter_wait(slot)

        @pl.when(nt >= 2)
        def _():
            tile_scatter_wait(1 - slot)


def _moe(xn_tiles, cls, wr, br, wg, wu, wd, tm):
    n = cls.shape[0]
    e1, e2, src_g, src_s = _moe_plan(cls, tm)
    nt = e1.shape[0]
    smem = lambda imap: pl.BlockSpec((None, 1, tm), imap, memory_space=pltpu.SMEM)
    const = lambda t, e1, e2: (0, 0)
    w1 = lambda shp: pl.BlockSpec((None,) + shp, lambda t, e1, e2: (e1[t], 0, 0))
    w2 = lambda shp: pl.BlockSpec((None,) + shp, lambda t, e1, e2: (e2[t], 0, 0))
    up, down = (D_MODEL, D_EXPERT), (D_EXPERT, D_MODEL)
    grid_spec = pltpu.PrefetchScalarGridSpec(
        num_scalar_prefetch=2,
        grid=(nt,),
        in_specs=[smem(lambda t, e1, e2: (t, 0, 0)),
                  smem(lambda t, e1, e2: (jnp.minimum(t + 1, nt - 1), 0, 0)),
                  smem(lambda t, e1, e2: (t, 0, 0)),
                  pl.BlockSpec(memory_space=pl.ANY),
                  pl.BlockSpec((D_MODEL, LANES), const),
                  pl.BlockSpec((1, LANES), const),
                  w1(up), w1(up), w1(down), w2(up), w2(up), w2(down)],
        out_specs=pl.BlockSpec(memory_space=pl.ANY),
        scratch_shapes=[pltpu.VMEM((2, tm * SUBLANES, LANES), F32),
                        pltpu.VMEM((2, tm * SUBLANES, LANES), F32),
                        pltpu.SemaphoreType.DMA((2,)),
                        pltpu.SemaphoreType.DMA((2,))])
    return pl.pallas_call(
        functools.partial(_moe_kernel, tm=tm, n=n),
        grid_spec=grid_spec,
        out_shape=jax.ShapeDtypeStruct(((n + 2 * tm) * SUBLANES, LANES), F32),
        compiler_params=_cparams(("arbitrary",)),
        name="moe",
    )(e1, e2, src_g, src_g, src_s, xn_tiles, wr, br, wg, wu, wd, wg, wu, wd)


def _residual_kernel(x1_ref, y_ref, *o_refs, starts):
    i = pl.program_id(0)
    ends = starts[1:] + (pl.num_programs(0),)
    out = x1_ref[...] + _from_token_tiles(y_ref, x1_ref.shape[0])
    for o_ref, end in zip(o_refs, ends):
        @pl.when(i < end)
        def _():
            o_ref[...] = out


def _residual(x1, y_tiles, tm, row_splits):
    row = lambda i: (i, 0)
    outs = [jax.ShapeDtypeStruct((r, D_MODEL), F32) for r in row_splits]
    ospecs, starts, ntiles = _row_source_specs(outs, tm)
    return pl.pallas_call(
        functools.partial(_residual_kernel, starts=starts),
        grid=(ntiles,),
        in_specs=[pl.BlockSpec((tm, D_MODEL), row), pl.BlockSpec((tm * SUBLANES, LANES), row)],
        out_specs=ospecs,
        out_shape=outs,
        compiler_params=_cparams(("arbitrary",)),
        name="moe_residual",
    )(x1, y_tiles)


def _tiles(nb, s):
    return dict(tm=min(512, s), tconv=min(512, s),
                tq=min(512, s), tk=min(2048, s), tmoe=min(256, s))


def _layer(xsrcs, nb, s, p, l, tables, tl, row_splits):
    n = nb * s
    gains = jnp.stack([jnp.tile(p[k][l], N_HEADS) for k in ("qn_c", "kn_c", "qn_d", "kn_d")])
    pa, pb, c1, c4, c16, qd, kd, vd = _inproj(xsrcs, p["norm1_g"][l], p["w_in"][l], gains, tables,
                                              nb, s, tl["tq"])
    pa = pa.reshape(nb, s, SZ_A)
    pb = pb.reshape(nb, s, SZ_B)
    ua, ub = _conv_mixers(pa, pb, p["conv_a_w"][l], p["conv_a_b"][l], p["ln_a_g"][l],
                          p["ln_a_b"][l], p["conv_b_w"][l], tl["tconv"])
    uc = _dilated_mixture((c1, c4, c16), tl["tm"])
    ud = _gqa(qd, kd, vd, tl["tq"], tl["tk"])
    x1, xn_tiles, cls = _outproj(ua.reshape(n, W_MIX), ub.reshape(n, W_MIX), uc.reshape(n, W_MIX),
                                 ud.reshape(n, W_MIX), xsrcs, p["w_out"][l], p["norm2_g"][l],
                                 p["w_route"][l], p["b_route"][l], tl["tm"])
    y = _moe(xn_tiles, cls[:, 0].astype(jnp.int32), p["w_route"][l], p["b_route"][l],
             p["w_gate"][l], p["w_up"][l], p["w_down"][l], tl["tmoe"])
    return _residual(x1, y, tl["tm"], row_splits)


def _trunk(xs, p):
    s = xs[0].shape[1]
    assert all(x.shape[1:] == (s, D_MODEL) for x in xs)
    nb = sum(x.shape[0] for x in xs)
    tables = _rope_tables(s)
    tl = _tiles(nb, s)
    depth = p["w_in"].shape[0]
    srcs = [x.reshape(-1, D_MODEL) for x in xs]
    for l in range(depth):
        splits = [x.shape[0] * s for x in xs] if l == depth - 1 else [nb * s]
        srcs = _layer(srcs, nb, s, p, l, tables, tl, splits)
    return tuple(y.reshape(x.shape) for y, x in zip(srcs, xs))


def _pack_params(norm1_g, w_in, conv_a_w, conv_a_b, ln_a_g, ln_a_b, conv_b_w, qn_c, kn_c, qn_d, kn_d,
                 w_out, norm2_g, w_group, b_group, w_router, b_router, w_gate, w_up, w_down):
    depth = w_in.shape[0]
    pad = LANES - N_GROUPS - N_EXPERTS
    w_route = jnp.concatenate([w_group, w_router, jnp.zeros((depth, D_MODEL, pad), F32)], axis=-1)
    b_route = jnp.concatenate([b_group, b_router, jnp.zeros((depth, pad), F32)], axis=-1)
    return dict(norm1_g=norm1_g, w_in=w_in.astype(BF16), conv_a_w=conv_a_w, conv_a_b=conv_a_b,
                ln_a_g=ln_a_g, ln_a_b=ln_a_b, conv_b_w=conv_b_w, qn_c=qn_c, kn_c=kn_c, qn_d=qn_d,
                kn_d=kn_d, w_out=w_out.astype(BF16), norm2_g=norm2_g,
                w_route=w_route.astype(BF16), b_route=b_route.reshape(depth, 1, LANES),
                w_gate=w_gate.astype(BF16), w_up=w_up.astype(BF16), w_down=w_down.astype(BF16))


def kernel(x_prompt, x_sample, norm1_g, w_in, conv_a_w, conv_a_b, ln_a_g, ln_a_b, conv_b_w, qn_c, kn_c, qn_d, kn_d, w_out, norm2_g, w_group, b_group, w_router, b_router, w_gate, w_up, w_down):
    p = _pack_params(norm1_g, w_in, conv_a_w, conv_a_b, ln_a_g, ln_a_b, conv_b_w, qn_c, kn_c, qn_d,
                     kn_d, w_out, norm2_g, w_group, b_group, w_router, b_router, w_gate, w_up, w_down)
    return _trunk((x_prompt, x_sample), p)
```

```python
import functools
import math

import jax
import jax.numpy as jnp
from jax import lax
from jax.experimental import pallas as pl
from jax.experimental.pallas import tpu as pltpu

D_MODEL = 1024
HEAD_DIM = 64
W_MIX = 256
N_HEADS = 4
N_KV_HEADS = 2
CONV_A_WIDTH = 31
CONV_B_WIDTH = 3
DIL_PATTERNS = ((128, 1), (512, 4), (2048, 16))
Q_BLOCK = 128
GRID_W = 64
ROPE_THETA = 10000.0
N_GROUPS = 4
EXPERTS_PER_GROUP = 4
N_EXPERTS = 16
PAIRS_PER_GROUP = EXPERTS_PER_GROUP * (EXPERTS_PER_GROUP - 1) // 2
N_CLASSES = N_GROUPS * PAIRS_PER_GROUP
D_EXPERT = 512
EPS = 1e-6
NEG_INF = -1e30
LOG2E = math.log2(math.e)

SZ_A, SZ_B, SZ_C, SZ_D = 512, 768, 768, 512
D_IN = SZ_A + SZ_B + SZ_C + SZ_D
LANES = 128
SUBLANES = 8
HALO = 16
VMEM_LIMIT = 56 * 1024 * 1024

F32 = jnp.float32
BF16 = jnp.bfloat16


def _cparams(sem):
    return pltpu.CompilerParams(dimension_semantics=sem, vmem_limit_bytes=VMEM_LIMIT)


def _tile_of_source(i, *_, start, count):
    return (jnp.clip(i - start, 0, count - 1), 0)


def _row_source_specs(srcs, tm):
    specs, starts, start = [], [], 0
    for a in srcs:
        count = a.shape[0] // tm
        specs.append(pl.BlockSpec((tm, a.shape[1]),
                                  functools.partial(_tile_of_source, start=start, count=count)))
        starts.append(start)
        start += count
    return specs, tuple(starts), start


def _pick_source(refs, starts, i):
    x = refs[0][...]
    for ref, start in zip(refs[1:], starts[1:]):
        x = jnp.where(i >= start, ref[...], x)
    return x


def _inproj_kernel(*refs, starts):
    ns = len(starts)
    (g_ref, w_ref, gains_ref, cosc_ref, sinc_ref, cosd_ref, sind_ref, ones_ref,
     pa_ref, pb_ref, c1_ref, c4_ref, c16_ref, qd_ref, kd_ref, vd_ref, qkv_scr) = refs[ns:]
    x = _pick_source(refs[:ns], starts, pl.program_id(0))
    ms = jnp.mean(x * x, axis=-1, keepdims=True)
    h = (x * lax.rsqrt(ms + EPS) * g_ref[...]).astype(BF16)

    def proj(lo, sz):
        return jnp.dot(h, w_ref[:, lo:lo + sz], preferred_element_type=F32)

    pa_ref[...] = proj(0, SZ_A)
    pb_ref[...] = proj(SZ_A, SZ_B)
    _prep_rows(proj(SZ_A + SZ_B, SZ_C), proj(SZ_A + SZ_B + SZ_C, SZ_D), gains_ref[...],
               cosc_ref[...], sinc_ref[...], cosd_ref[...], sind_ref[...], ones_ref[...],
               (c1_ref, c4_ref, c16_ref), qd_ref, kd_ref, vd_ref, qkv_scr)


def _inproj(xsrcs, g, w_bf16, gains, tables, nb, s, tm):
    xspecs, starts, ntiles = _row_source_specs(xsrcs, tm)
    n = ntiles * tm
    tps = s // tm
    cos_c, sin_c, cos_d, sin_d = tables
    head = lax.broadcasted_iota(jnp.int32, (W_MIX, W_MIX), 0) // HEAD_DIM
    ones_bd = (head == head.T).astype(BF16)
    group = N_HEADS // N_KV_HEADS
    const = lambda i: (0, 0)
    row = lambda i: (i, 0)
    tab = lambda i: (i % tps, 0)
    hm = lambda i: (i // tps, 0, i % tps, 0)
    return pl.pallas_call(
        functools.partial(_inproj_kernel, starts=starts),
        grid=(ntiles,),
        in_specs=xspecs + [pl.BlockSpec((1, D_MODEL), const),
                           pl.BlockSpec((D_MODEL, D_IN), const),
                           pl.BlockSpec((4, W_MIX), const),
                           pl.BlockSpec((tm, W_MIX), tab),
                           pl.BlockSpec((tm, W_MIX), tab),
                           pl.BlockSpec((tm, W_MIX), tab),
                           pl.BlockSpec((tm, W_MIX), tab),
                           pl.BlockSpec((W_MIX, W_MIX), const)],
        out_specs=[pl.BlockSpec((tm, SZ_A), row), pl.BlockSpec((tm, SZ_B), row)]
        + [pl.BlockSpec((None, dil, tm // dil, SZ_C), hm) for _, dil in DIL_PATTERNS]
        + [pl.BlockSpec((None, N_KV_HEADS, None, group, tm, HEAD_DIM),
                        lambda i: (i // tps, 0, i % tps, 0, 0, 0)),
           pl.BlockSpec((None, N_KV_HEADS, HEAD_DIM, tm), lambda i: (i // tps, 0, 0, i % tps)),
           pl.BlockSpec((None, N_KV_HEADS, tm, LANES), hm)],
        out_shape=[jax.ShapeDtypeStruct((n, SZ_A), F32), jax.ShapeDtypeStruct((n, SZ_B), F32)]
        + [jax.ShapeDtypeStruct((nb, dil, s // dil, SZ_C), BF16) for _, dil in DIL_PATTERNS]
        + [jax.ShapeDtypeStruct((nb, N_KV_HEADS, tps, group, tm, HEAD_DIM), BF16),
           jax.ShapeDtypeStruct((nb, N_KV_HEADS, HEAD_DIM, s), BF16),
           jax.ShapeDtypeStruct((nb, N_KV_HEADS, s, LANES), BF16)],
        scratch_shapes=[pltpu.VMEM((SZ_C // LANES, tm, LANES), F32)],
        compiler_params=_cparams(("parallel",)),
        name="inproj",
    )(*xsrcs, g.reshape(1, D_MODEL), w_bf16, gains, cos_c, sin_c, cos_d, sin_d, ones_bd)


def _conv_kernel(pa_ref, pa_prev_ref, pa_next_ref, pb_ref, pb_prev_ref, pb_next_ref,
                 wa_ref, ba_ref, lg_ref, lb_ref, wb_ref, ua_ref, ub_ref, bufa, bufb, *, t):
    si = pl.program_id(1)
    ns = pl.num_programs(1)
    has_prev = (si > 0).astype(F32)
    has_next = (si < ns - 1).astype(F32)

    def glu(ref):
        p = ref[...]
        return p[:, :W_MIX] * jax.nn.sigmoid(p[:, W_MIX:])

    def gated(ref):
        p = ref[...]
        return p[:, W_MIX:2 * W_MIX] * p[:, 2 * W_MIX:]

    bufa[0:HALO, :] = glu(pa_prev_ref) * has_prev
    bufa[HALO:HALO + t, :] = glu(pa_ref)
    bufa[HALO + t:2 * HALO + t, :] = glu(pa_next_ref) * has_next
    bufb[0:HALO, :] = gated(pb_prev_ref) * has_prev
    bufb[HALO:HALO + t, :] = gated(pb_ref)
    bufb[HALO + t:2 * HALO + t, :] = gated(pb_next_ref) * has_next

    first = HALO - CONV_A_WIDTH // 2
    acc = jnp.zeros((t, W_MIX), F32)
    for b in range(SUBLANES):
        z = None
        for a in range((2 * HALO) // SUBLANES):
            k = SUBLANES * a + b - first
            if 0 <= k < CONV_A_WIDTH:
                term = bufa[SUBLANES * a:SUBLANES * a + t + SUBLANES, :] * wa_ref[k:k + 1, :]
                z = term if z is None else z + term
        acc = acc + z[b:b + t, :]
    acc = acc + ba_ref[...]
    mu = jnp.mean(acc, axis=-1, keepdims=True)
    xc = acc - mu
    var = jnp.mean(xc * xc, axis=-1, keepdims=True)
    y = xc * lax.rsqrt(var + EPS) * lg_ref[...] + lb_ref[...]
    ua_ref[...] = (y * jax.nn.sigmoid(y)).astype(BF16)

    accb = jnp.zeros((t, W_MIX), F32)
    for k in range(CONV_B_WIDTH):
        start = HALO - CONV_B_WIDTH // 2 + k
        accb = accb + bufb[start:start + t, :] * wb_ref[k:k + 1, :]
    ub_ref[...] = (pb_ref[:, :W_MIX] * accb).astype(BF16)


def _conv_mixers(pa, pb, wa, ba, lg, lb, wb, t):
    nb, s, _ = pa.shape
    hb = t // HALO
    last = s // HALO - 1

    def main(b, i):
        return (b, i, 0)

    def prev(b, i):
        return (b, jnp.maximum(i * hb - 1, 0), 0)

    def nxt(b, i):
        return (b, jnp.minimum((i + 1) * hb, last), 0)

    const = lambda b, i: (0, 0)
    return pl.pallas_call(
        functools.partial(_conv_kernel, t=t),
        grid=(nb, s // t),
        in_specs=[pl.BlockSpec((None, t, SZ_A), main),
                  pl.BlockSpec((None, HALO, SZ_A), prev),
                  pl.BlockSpec((None, HALO, SZ_A), nxt),
                  pl.BlockSpec((None, t, SZ_B), main),
                  pl.BlockSpec((None, HALO, SZ_B), prev),
                  pl.BlockSpec((None, HALO, SZ_B), nxt),
                  pl.BlockSpec((CONV_A_WIDTH, W_MIX), const),
                  pl.BlockSpec((1, W_MIX), const),
                  pl.BlockSpec((1, W_MIX), const),
                  pl.BlockSpec((1, W_MIX), const),
                  pl.BlockSpec((CONV_B_WIDTH, W_MIX), const)],
        out_specs=[pl.BlockSpec((None, t, W_MIX), main),
                   pl.BlockSpec((None, t, W_MIX), main)],
        out_shape=[jax.ShapeDtypeStruct((nb, s, W_MIX), BF16)] * 2,
        scratch_shapes=[pltpu.VMEM((t + 2 * HALO, W_MIX), F32),
                        pltpu.VMEM((t + 2 * HALO, W_MIX), F32)],
        compiler_params=_cparams(("parallel", "parallel")),
        name="conv_mixers",
    )(pa, pa, pa, pb, pb, pb, wa, ba.reshape(1, W_MIX), lg.reshape(1, W_MIX),
      lb.reshape(1, W_MIX), wb)


def _head_sumsq(x, ones_bd):
    x2 = x * x
    hi = x2.astype(BF16)
    lo = (x2 - hi.astype(F32)).astype(BF16)
    return (jnp.dot(hi, ones_bd, preferred_element_type=F32)
            + jnp.dot(lo, ones_bd, preferred_element_type=F32))


def _rope(x, cos, sin_signed, half):
    w = x.shape[-1]
    lane = lax.broadcasted_iota(jnp.int32, x.shape, 1)
    fwd = pltpu.roll(x, w - half, 1)
    bwd = pltpu.roll(x, half, 1)
    rot = jnp.where(lane % (2 * half) < half, fwd, bwd)
    return x * cos + rot * sin_signed


def _prep_rows(pc, pd, gains, cosc, sinc, cosd, sind, ones_bd, c_refs, qd_ref, kd_ref, vd_ref,
               qkv_scr):
    scale = HEAD_DIM ** -0.5

    def normed(x, g, ones):
        ss = _head_sumsq(x, ones)
        return x * lax.rsqrt(ss * (1.0 / HEAD_DIM) + EPS) * g

    qc = _rope(normed(pc[:, 0:256], gains[0:1, :], ones_bd), cosc, sinc, HEAD_DIM // 2) * (scale * LOG2E)
    kc = _rope(normed(pc[:, 256:512], gains[1:2, :], ones_bd), cosc, sinc, HEAD_DIM // 2)
    qkv = (qc, kc, pc[:, 512:768])
    ncb = SZ_C // LANES
    for c in range(ncb):
        part = qkv[c * LANES // W_MIX]
        lo = c * LANES % W_MIX
        qkv_scr[c] = part[:, lo:lo + LANES]
    ts = pc.shape[0]
    for ref, (_, dil) in zip(c_refs, DIL_PATTERNS):
        for r in range(dil):
            for c in range(ncb):
                ref[r, :, c * LANES:(c + 1) * LANES] = (
                    qkv_scr[c, pl.ds(r, ts // dil, stride=dil), :].astype(BF16))

    qd = _rope(normed(pd[:, 0:256], gains[2:3, :], ones_bd), cosd, sind, HEAD_DIM // 4) * (scale * LOG2E)
    kd = _rope(normed(pd[:, 256:384], gains[3:4, 0:128], ones_bd[0:128, 0:128]),
               cosd[:, 0:128], sind[:, 0:128], HEAD_DIM // 4)
    vd = pd[:, 384:512]
    group = N_HEADS // N_KV_HEADS
    for h in range(N_HEADS):
        qd_ref[h // group, h % group] = qd[:, h * HEAD_DIM:(h + 1) * HEAD_DIM].astype(BF16)
    lane = lax.broadcasted_iota(jnp.int32, vd.shape, 1)
    kd_t = kd.T.astype(BF16)
    for h in range(N_KV_HEADS):
        kd_ref[h] = kd_t[h * HEAD_DIM:(h + 1) * HEAD_DIM, :]
        vh = vd if h == 0 else pltpu.roll(vd, LANES - h * HEAD_DIM, 1)
        vd_ref[h] = jnp.where(lane < HEAD_DIM, vh, jnp.where(lane == HEAD_DIM, 1.0, 0.0)).astype(BF16)


def _rope_tables(s):
    def angles(pos, dim):
        inv = ROPE_THETA ** (-jnp.arange(0, dim, 2, dtype=F32) / dim)
        return pos.astype(F32)[:, None] * inv[None, :]

    pos = jnp.arange(s)
    ang = angles(pos, HEAD_DIM)
    cos_c = jnp.tile(jnp.concatenate([jnp.cos(ang), jnp.cos(ang)], -1), (1, N_HEADS))
    sin_c = jnp.tile(jnp.concatenate([-jnp.sin(ang), jnp.sin(ang)], -1), (1, N_HEADS))
    row = pos // GRID_W
    col = pos % GRID_W
    ar = angles(row, HEAD_DIM // 2)
    ac = angles(col, HEAD_DIM // 2)
    cos_d = jnp.tile(jnp.concatenate([jnp.cos(ar), jnp.cos(ar), jnp.cos(ac), jnp.cos(ac)], -1),
                     (1, N_HEADS))
    sin_d = jnp.tile(jnp.concatenate([-jnp.sin(ar), jnp.sin(ar), -jnp.sin(ac), jnp.sin(ac)], -1),
                     (1, N_HEADS))
    return cos_c, sin_c, cos_d, sin_d


def _dilated_kernel(q_ref, k_ref, v_ref, o_ref, lse_ref, *, seq_sub, half, lq):
    span = Q_BLOCK + 2 * half
    qi = pl.program_id(2)
    head_of_row = lax.broadcasted_iota(jnp.int32, (W_MIX, Q_BLOCK), 0) // HEAD_DIM
    for i in range(lq // Q_BLOCK):
        q0 = i * Q_BLOCK
        qabs = qi * lq + q0
        ks = pl.multiple_of(jnp.clip(qabs - half, 0, seq_sub - span), half)
        qt = q_ref[q0:q0 + Q_BLOCK, :].astype(F32).T
        qbd = jnp.concatenate([jnp.where(head_of_row == h, qt, 0.0) for h in range(N_HEADS)],
                              axis=1).astype(BF16)
        k = k_ref[pl.ds(ks, span), :]
        vt = v_ref[pl.ds(ks, span), :].astype(F32).T.astype(BF16)
        st = jnp.dot(k, qbd, preferred_element_type=F32)
        kpos = ks + lax.broadcasted_iota(jnp.int32, (span, Q_BLOCK), 0)
        qpos = qabs + lax.broadcasted_iota(jnp.int32, (span, Q_BLOCK), 1)
        mask = jnp.abs(kpos - qpos) <= half
        pts, ls, lses = [], [], []
        for h in range(N_HEADS):
            sc = jnp.where(mask, st[:, h * Q_BLOCK:(h + 1) * Q_BLOCK], NEG_INF)
            m = jnp.max(sc, axis=0, keepdims=True)
            p = jnp.exp2(sc - m)
            l = jnp.sum(p, axis=0, keepdims=True)
            pts.append(p.astype(BF16))
            ls.append(l)
            lses.append(m + jnp.log2(l))
        ot_all = jnp.dot(vt, jnp.concatenate(pts, axis=1), preferred_element_type=F32)
        ot = jnp.concatenate(
            [ot_all[h * HEAD_DIM:(h + 1) * HEAD_DIM, h * Q_BLOCK:(h + 1) * Q_BLOCK] / ls[h]
             for h in range(N_HEADS)], axis=0)
        lset = jnp.concatenate([jnp.broadcast_to(x, (HEAD_DIM, Q_BLOCK)) for x in lses], axis=0)
        o_ref[q0:q0 + Q_BLOCK, :] = ot.T
        lse_ref[q0:q0 + Q_BLOCK, :] = lset.T


def _dilated_pattern(qkv, win, dil):
    nb, _, seq_sub, _ = qkv.shape
    half = win // (2 * dil)
    lq = min(seq_sub, 1024)
    qspec = pl.BlockSpec((None, None, lq, W_MIX), lambda b, r, i: (b, r, i, 0))
    kspec = pl.BlockSpec((None, None, seq_sub, W_MIX), lambda b, r, i: (b, r, 0, 1))
    vspec = pl.BlockSpec((None, None, seq_sub, W_MIX), lambda b, r, i: (b, r, 0, 2))
    return pl.pallas_call(
        functools.partial(_dilated_kernel, seq_sub=seq_sub, half=half, lq=lq),
        grid=(nb, dil, seq_sub // lq),
        in_specs=[qspec, kspec, vspec],
        out_specs=[qspec, qspec],
        out_shape=[jax.ShapeDtypeStruct((nb, dil, seq_sub, W_MIX), F32)] * 2,
        compiler_params=_cparams(("parallel", "parallel", "arbitrary")),
        name=f"dilated_w{win}_d{dil}",
    )(qkv, qkv, qkv)


def _mixture_kernel(*refs, ts):
    n = len(DIL_PATTERNS)
    o_refs, l_refs = refs[0:2 * n:2], refs[1:2 * n:2]
    uc_ref = refs[2 * n]
    scr = refs[2 * n + 1:]

    def natural(ref, dil, buf):
        if dil == 1:
            return ref[0]
        ncb = W_MIX // LANES
        for r in range(dil):
            for c in range(ncb):
                buf[c, pl.ds(r, ts // dil, stride=dil), :] = ref[r, :, c * LANES:(c + 1) * LANES]
        return jnp.concatenate([buf[c] for c in range(ncb)], axis=-1)

    os_, ls_ = [], []
    for j, (_, dil) in enumerate(DIL_PATTERNS):
        os_.append(natural(o_refs[j], dil, scr[2 * j]))
        ls_.append(natural(l_refs[j], dil, scr[2 * j + 1]))
    mx = functools.reduce(jnp.maximum, ls_)
    ws = [jnp.exp2(l - mx) for l in ls_]
    num = sum(w * o for w, o in zip(ws, os_))
    uc_ref[...] = (num / sum(ws)).astype(uc_ref.dtype)


def _dilated_mixture(c_layouts, ts):
    nb, _, s, _ = c_layouts[0].shape
    ins, in_specs = [], []
    for qkv, (win, dil) in zip(c_layouts, DIL_PATTERNS):
        o, lse = _dilated_pattern(qkv, win, dil)
        spec = pl.BlockSpec((None, dil, ts // dil, W_MIX), lambda b, i: (b, 0, i, 0))
        ins += [o, lse]
        in_specs += [spec, spec]
    return pl.pallas_call(
        functools.partial(_mixture_kernel, ts=ts),
        grid=(nb, s // ts),
        in_specs=in_specs,
        out_specs=pl.BlockSpec((None, ts, W_MIX), lambda b, i: (b, i, 0)),
        out_shape=jax.ShapeDtypeStruct((nb, s, W_MIX), BF16),
        scratch_shapes=[pltpu.VMEM((W_MIX // LANES, ts, LANES), F32)] * (2 * len(DIL_PATTERNS)),
        compiler_params=_cparams(("parallel", "parallel")),
        name="dilated_mixture",
    )(*ins)


def _gqa_kernel(q_ref, k_ref, v_ref, o_ref, *, tq, tk, s):
    group = N_HEADS // N_KV_HEADS
    rows = group * tq
    qs = [q_ref[g] for g in range(N_KV_HEADS)]

    def body(c, carry):
        k0 = pl.multiple_of(c * tk, tk)
        new = []
        for g in range(N_KV_HEADS):
            kt = k_ref[g, :, pl.ds(k0, tk)]
            v = v_ref[g, pl.ds(k0, tk), :]
            m, acc = carry[g]
            sc = jnp.dot(qs[g], kt, preferred_element_type=F32)
            m_new = jnp.maximum(m, jnp.max(sc, axis=-1, keepdims=True))
            alpha = jnp.exp2(m - m_new)
            p = jnp.exp2(sc - m_new).astype(BF16)
            acc = alpha * acc + jnp.dot(p, v, preferred_element_type=F32)
            new.append((m_new, acc))
        return tuple(new)

    init = tuple((jnp.full((rows, 1), -jnp.inf, F32), jnp.zeros((rows, LANES), F32))
                 for _ in range(N_KV_HEADS))
    fin = lax.fori_loop(0, s // tk, body, init, unroll=True)
    outs = []
    for (_, acc) in fin:
        o = acc[:, :HEAD_DIM] / acc[:, HEAD_DIM:HEAD_DIM + 1]
        outs += [o[j * tq:(j + 1) * tq, :] for j in range(group)]
    o_ref[...] = jnp.concatenate(outs, axis=-1).astype(o_ref.dtype)


def _gqa(qd, kd, vd, tq, tk):
    nb, _, nq, group, _, _ = qd.shape
    s = nq * tq
    qd = qd.reshape(nb, N_KV_HEADS, nq, group * tq, HEAD_DIM)
    return pl.pallas_call(
        functools.partial(_gqa_kernel, tq=tq, tk=tk, s=s),
        grid=(nb, nq),
        in_specs=[pl.BlockSpec((None, N_KV_HEADS, None, group * tq, HEAD_DIM),
                               lambda b, i: (b, 0, i, 0, 0)),
                  pl.BlockSpec((None, N_KV_HEADS, HEAD_DIM, s), lambda b, i: (b, 0, 0, 0)),
                  pl.BlockSpec((None, N_KV_HEADS, s, LANES), lambda b, i: (b, 0, 0, 0))],
        out_specs=pl.BlockSpec((None, tq, W_MIX), lambda b, i: (b, i, 0)),
        out_shape=jax.ShapeDtypeStruct((nb, s, W_MIX), BF16),
        compiler_params=_cparams(("parallel", "arbitrary")),
        name="gqa",
    )(qd, kd, vd)


def _outproj_kernel(*refs, starts):
    ns = len(starts)
    ua_ref, ub_ref, uc_ref, ud_ref = refs[:4]
    w_ref, g_ref, wr_ref, br_ref, x1_ref, xn_ref, cls_ref = refs[4 + ns:]
    acc = _pick_source(refs[4:4 + ns], starts, pl.program_id(0))
    for j, ref in enumerate((ua_ref, ub_ref, uc_ref, ud_ref)):
        acc = acc + jnp.dot(ref[...], w_ref[j * W_MIX:(j + 1) * W_MIX, :],
                            preferred_element_type=F32)
    x1_ref[...] = acc
    ms = jnp.mean(acc * acc, axis=-1, keepdims=True)
    xn = acc * lax.rsqrt(ms + EPS) * g_ref[...]
    logits = jnp.dot(xn.astype(BF16), wr_ref[...], preferred_element_type=F32) + br_ref[...]
    _to_token_tiles(xn_ref, xn)
    cls_ref[...] = jnp.broadcast_to(_route_class(logits).astype(F32), cls_ref.shape)


def _to_token_tiles(ref, x):
    t = x.shape[0]
    for j in range(D_MODEL // LANES):
        ref[pl.ds(j, t, stride=SUBLANES), :] = x[:, j * LANES:(j + 1) * LANES]


def _from_token_tiles(ref, t):
    return jnp.concatenate([ref[pl.ds(j, t, stride=SUBLANES), :] for j in range(D_MODEL // LANES)],
                           axis=1)


def _route_class(logits):
    lane = lax.broadcasted_iota(jnp.int32, logits.shape, 1)
    big = jnp.int32(LANES)
    ninf = -jnp.inf

    def argmax(vals):
        mx = jnp.max(vals, axis=-1, keepdims=True)
        return jnp.min(jnp.where(vals == mx, lane, big), axis=-1, keepdims=True)

    gidx = argmax(jnp.where(lane < N_GROUPS, logits, ninf))
    lo = N_GROUPS + gidx * EXPERTS_PER_GROUP
    el = jnp.where((lane >= lo) & (lane < lo + EXPERTS_PER_GROUP), logits, ninf)
    i1 = argmax(el)
    i2 = argmax(jnp.where(lane == i1, ninf, el))
    a = jnp.minimum(i1, i2) - lo
    b = jnp.maximum(i1, i2) - lo
    pair = lax.shift_right_logical(a * (2 * EXPERTS_PER_GROUP - 1 - a), 1) + (b - a - 1)
    return gidx * PAIRS_PER_GROUP + pair


def _pair_gates(logits, e1, e2):
    lane = lax.broadcasted_iota(jnp.int32, logits.shape, 1)
    pick = lambda idx: jnp.sum(jnp.where(lane == idx, logits, 0.0), axis=-1, keepdims=True)
    glog = jnp.where(lane < N_GROUPS, logits, -jnp.inf)
    gmax = jnp.max(glog, axis=-1, keepdims=True)
    g_w = jnp.exp(pick(e1 // EXPERTS_PER_GROUP) - gmax) / jnp.sum(jnp.exp(glog - gmax), axis=-1,
                                                                    keepdims=True)
    l1, l2 = pick(N_GROUPS + e1), pick(N_GROUPS + e2)
    mx = jnp.maximum(l1, l2)
    a1, a2 = jnp.exp(l1 - mx), jnp.exp(l2 - mx)
    inv = g_w / (a1 + a2)
    return a1 * inv, a2 * inv


def _outproj(ua, ub, uc, ud, xsrcs, w_bf16, g2, wr, br, tm):
    xspecs, starts, ntiles = _row_source_specs(xsrcs, tm)
    n = ntiles * tm
    row = lambda i: (i, 0)
    const = lambda i: (0, 0)
    return pl.pallas_call(
        functools.partial(_outproj_kernel, starts=starts),
        grid=(ntiles,),
        in_specs=[pl.BlockSpec((tm, W_MIX), row)] * 4 + xspecs
        + [pl.BlockSpec((D_MODEL, D_MODEL), const),
           pl.BlockSpec((1, D_MODEL), const),
           pl.BlockSpec((D_MODEL, LANES), const),
           pl.BlockSpec((1, LANES), const)],
        out_specs=[pl.BlockSpec((tm, D_MODEL), row),
                   pl.BlockSpec((tm * SUBLANES, LANES), row),
                   pl.BlockSpec((tm, LANES), row)],
        out_shape=[jax.ShapeDtypeStruct((n, D_MODEL), F32),
                   jax.ShapeDtypeStruct((n * SUBLANES, LANES), F32),
                   jax.ShapeDtypeStruct((n, LANES), F32)],
        compiler_params=_cparams(("parallel",)),
        name="outproj_router",
    )(ua, ub, uc, ud, *xsrcs, w_bf16, g2.reshape(1, D_MODEL), wr, br)


def _class_experts():
    e1, e2 = [], []
    for g in range(N_GROUPS):
        for a in range(EXPERTS_PER_GROUP):
            for b in range(a + 1, EXPERTS_PER_GROUP):
                e1.append(g * EXPERTS_PER_GROUP + a)
                e2.append(g * EXPERTS_PER_GROUP + b)
    return jnp.array(e1, jnp.int32), jnp.array(e2, jnp.int32)


def _moe_plan(cls, tm):
    n = cls.shape[0]
    nt = n // tm + N_CLASSES
    classes = jnp.arange(N_CLASSES, dtype=jnp.int32)
    counts = jnp.sum((cls[:, None] == classes[None, :]).astype(jnp.int32), axis=0)
    _, order = lax.sort((cls, jnp.arange(n, dtype=jnp.int32)), num_keys=1, is_stable=True)
    cstart = jnp.cumsum(counts) - counts
    ctiles = (counts + tm - 1) // tm
    tstart = jnp.cumsum(ctiles) - ctiles
    t = jnp.arange(nt, dtype=jnp.int32)
    tclass = jnp.clip(jnp.sum((t[:, None] >= tstart[None, :]).astype(jnp.int32), axis=1) - 1,
                      0, N_CLASSES - 1)
    p = jnp.arange(nt * tm, dtype=jnp.int32)
    room = N_CLASSES * tm
    ext = jnp.concatenate([jnp.zeros((room,), jnp.int32), order, jnp.zeros((room,), jnp.int32)])
    src = jnp.full((nt * tm,), -1, jnp.int32)
    for c in range(N_CLASSES):
        lo = tstart[c] * tm
        shifted = lax.dynamic_slice(ext, (room - (lo - cstart[c]),), (nt * tm,))
        src = jnp.where((p >= lo) & (p < lo + counts[c]), shifted, src)
    trash = n + p % (2 * tm)
    src_g = jnp.where(src < 0, 0, src).reshape(nt, 1, tm)
    src_s = jnp.where(src < 0, trash, src).reshape(nt, 1, tm)
    e1, e2 = _class_experts()
    return e1[tclass], e2[tclass], src_g, src_s


def _moe_kernel(e1_ref, e2_ref, srcg_cur_ref, srcg_nxt_ref, srcs_ref, xn_hbm, wr_ref, br_ref,
                wg1_ref, wu1_ref, wd1_ref, wg2_ref, wu2_ref, wd2_ref,
                y_hbm, xg, ys, gsem, ssem, *, tm, n):
    t = pl.program_id(0)
    nt = pl.num_programs(0)
    slot = t % 2
    rows = tm * SUBLANES

    def token(ref, tok):
        start = tok * SUBLANES
        if not isinstance(tok, int):
            start = pl.multiple_of(start, SUBLANES)
        return ref.at[pl.ds(start, SUBLANES)]

    def gather(src_ref, dst_slot):
        for r in range(tm):
            pltpu.make_async_copy(token(xn_hbm, src_ref[0, r]), token(xg.at[dst_slot], r),
                                  gsem.at[dst_slot]).start(priority=r % 2)

    def tile_gather_wait(s_):
        pltpu.make_async_copy(xn_hbm.at[pl.ds(0, rows)], xg.at[s_], gsem.at[s_]).wait()

    def tile_scatter_wait(s_):
        pltpu.make_async_copy(ys.at[s_], y_hbm.at[pl.ds(0, rows)], ssem.at[s_]).wait()

    @pl.when(t == 0)
    def _():
        gather(srcg_cur_ref, 0)
        ys[1] = jnp.zeros((rows, LANES), F32)
        for half in range(2):
            cp = pltpu.make_async_copy(ys.at[1], y_hbm.at[pl.ds((n + half * tm) * SUBLANES, rows)],
                                       ssem.at[1])
            cp.start()
            cp.wait()

    tile_gather_wait(slot)

    @pl.when(t >= 2)
    def _():
        tile_scatter_wait(slot)

    gather(srcg_nxt_ref, 1 - slot)

    x = _from_token_tiles(xg.at[slot], tm).astype(BF16)
    logits = jnp.dot(x, wr_ref[...], preferred_element_type=F32) + br_ref[...]
    gates = _pair_gates(logits, e1_ref[t], e2_ref[t])
    y = jnp.zeros((tm, D_MODEL), F32)
    for ge, wg_ref, wu_ref, wd_ref in ((gates[0], wg1_ref, wu1_ref, wd1_ref),
                                       (gates[1], wg2_ref, wu2_ref, wd2_ref)):
        a = jnp.dot(x, wg_ref[...], preferred_element_type=F32)
        u = jnp.dot(x, wu_ref[...], preferred_element_type=F32)
        hdn = (a * jax.nn.sigmoid(a) * u).astype(BF16)
        y = y + ge * jnp.dot(hdn, wd_ref[...], preferred_element_type=F32)
    _to_token_tiles(ys.at[slot], y)

    for r in range(tm):
        pltpu.make_async_copy(token(ys.at[slot], r), token(y_hbm, srcs_ref[0, r]),
                              ssem.at[slot]).start(priority=r % 2)

    @pl.when(t == nt - 1)
    def _():
        tile_gather_wait(1 - slot)
        tile_scatter_wait(slot)

        @pl.when(nt >= 2)
        def _():
            tile_scatter_wait(1 - slot)


def _moe(xn_tiles, cls, wr, br, wg, wu, wd, tm):
    n = cls.shape[0]
    e1, e2, src_g, src_s = _moe_plan(cls, tm)
    nt = e1.shape[0]
    smem = lambda imap: pl.BlockSpec((None, 1, tm), imap, memory_space=pltpu.SMEM)
    const = lambda t, e1, e2: (0, 0)
    w1 = lambda shp: pl.BlockSpec((None,) + shp, lambda t, e1, e2: (e1[t], 0, 0))
    w2 = lambda shp: pl.BlockSpec((None,) + shp, lambda t, e1, e2: (e2[t], 0, 0))
    up, down = (D_MODEL, D_EXPERT), (D_EXPERT, D_MODEL)
    grid_spec = pltpu.PrefetchScalarGridSpec(
        num_scalar_prefetch=2,
        grid=(nt,),
        in_specs=[smem(lambda t, e1, e2: (t, 0, 0)),
                  smem(lambda t, e1, e2: (jnp.minimum(t + 1, nt - 1), 0, 0)),
                  smem(lambda t, e1, e2: (t, 0, 0)),
                  pl.BlockSpec(memory_space=pl.ANY),
                  pl.BlockSpec((D_MODEL, LANES), const),
                  pl.BlockSpec((1, LANES), const),
                  w1(up), w1(up), w1(down), w2(up), w2(up), w2(down)],
        out_specs=pl.BlockSpec(memory_space=pl.ANY),
        scratch_shapes=[pltpu.VMEM((2, tm * SUBLANES, LANES), F32),
                        pltpu.VMEM((2, tm * SUBLANES, LANES), F32),
                        pltpu.SemaphoreType.DMA((2,)),
                        pltpu.SemaphoreType.DMA((2,))])
    return pl.pallas_call(
        functools.partial(_moe_kernel, tm=tm, n=n),
        grid_spec=grid_spec,
        out_shape=jax.ShapeDtypeStruct(((n + 2 * tm) * SUBLANES, LANES), F32),
        compiler_params=_cparams(("arbitrary",)),
        name="moe",
    )(e1, e2, src_g, src_g, src_s, xn_tiles, wr, br, wg, wu, wd, wg, wu, wd)


def _residual_kernel(x1_ref, y_ref, *o_refs, starts):
    i = pl.program_id(0)
    ends = starts[1:] + (pl.num_programs(0),)
    out = x1_ref[...] + _from_token_tiles(y_ref, x1_ref.shape[0])
    for o_ref, end in zip(o_refs, ends):
        @pl.when(i < end)
        def _():
            o_ref[...] = out


def _residual(x1, y_tiles, tm, row_splits):
    row = lambda i: (i, 0)
    outs = [jax.ShapeDtypeStruct((r, D_MODEL), F32) for r in row_splits]
    ospecs, starts, ntiles = _row_source_specs(outs, tm)
    return pl.pallas_call(
        functools.partial(_residual_kernel, starts=starts),
        grid=(ntiles,),
        in_specs=[pl.BlockSpec((tm, D_MODEL), row), pl.BlockSpec((tm * SUBLANES, LANES), row)],
        out_specs=ospecs,
        out_shape=outs,
        compiler_params=_cparams(("arbitrary",)),
        name="moe_residual",
    )(x1, y_tiles)


def _tiles(nb, s):
    return dict(tm=min(512, s), tconv=min(512, s),
                tq=min(512, s), tk=min(2048, s), tmoe=min(256, s))


def _layer(xsrcs, nb, s, p, l, tables, tl, row_splits):
    n = nb * s
    gains = jnp.stack([jnp.tile(p[k][l], N_HEADS) for k in ("qn_c", "kn_c", "qn_d", "kn_d")])
    pa, pb, c1, c4, c16, qd, kd, vd = _inproj(xsrcs, p["norm1_g"][l], p["w_in"][l], gains, tables,
                                              nb, s, tl["tq"])
    pa = pa.reshape(nb, s, SZ_A)
    pb = pb.reshape(nb, s, SZ_B)
    ua, ub = _conv_mixers(pa, pb, p["conv_a_w"][l], p["conv_a_b"][l], p["ln_a_g"][l],
                          p["ln_a_b"][l], p["conv_b_w"][l], tl["tconv"])
    uc = _dilated_mixture((c1, c4, c16), tl["tm"])
    ud = _gqa(qd, kd, vd, tl["tq"], tl["tk"])
    x1, xn_tiles, cls = _outproj(ua.reshape(n, W_MIX), ub.reshape(n, W_MIX), uc.reshape(n, W_MIX),
                                 ud.reshape(n, W_MIX), xsrcs, p["w_out"][l], p["norm2_g"][l],
                                 p["w_route"][l], p["b_route"][l], tl["tm"])
    y = _moe(xn_tiles, cls[:, 0].astype(jnp.int32), p["w_route"][l], p["b_route"][l],
             p["w_gate"][l], p["w_up"][l], p["w_down"][l], tl["tmoe"])
    return _residual(x1, y, tl["tm"], row_splits)


def _trunk(xs, p):
    s = xs[0].shape[1]
    assert all(x.shape[1:] == (s, D_MODEL) for x in xs)
    nb = sum(x.shape[0] for x in xs)
    tables = _rope_tables(s)
    tl = _tiles(nb, s)
    depth = p["w_in"].shape[0]
    srcs = [x.reshape(-1, D_MODEL) for x in xs]
    for l in range(depth):
        splits = [x.shape[0] * s for x in xs] if l == depth - 1 else [nb * s]
        srcs = _layer(srcs, nb, s, p, l, tables, tl, splits)
    return tuple(y.reshape(x.shape) for y, x in zip(srcs, xs))


def _pack_params(norm1_g, w_in, conv_a_w, conv_a_b, ln_a_g, ln_a_b, conv_b_w, qn_c, kn_c, qn_d, kn_d,
                 w_out, norm2_g, w_group, b_group, w_router, b_router, w_gate, w_up, w_down):
    depth = w_in.shape[0]
    pad = LANES - N_GROUPS - N_EXPERTS
    w_route = jnp.concatenate([w_group, w_router, jnp.zeros((depth, D_MODEL, pad), F32)], axis=-1)
    b_route = jnp.concatenate([b_group, b_router, jnp.zeros((depth, pad), F32)], axis=-1)
    return dict(norm1_g=norm1_g, w_in=w_in.astype(BF16), conv_a_w=conv_a_w, conv_a_b=conv_a_b,
                ln_a_g=ln_a_g, ln_a_b=ln_a_b, conv_b_w=conv_b_w, qn_c=qn_c, kn_c=kn_c, qn_d=qn_d,
                kn_d=kn_d, w_out=w_out.astype(BF16), norm2_g=norm2_g,
                w_route=w_route.astype(BF16), b_route=b_route.reshape(depth, 1, LANES),
                w_gate=w_gate.astype(BF16), w_up=w_up.astype(BF16), w_down=w_down.astype(BF16))


def kernel(x_prompt, x_sample, norm1_g, w_in, conv_a_w, conv_a_b, ln_a_g, ln_a_b, conv_b_w, qn_c, kn_c, qn_d, kn_d, w_out, norm2_g, w_group, b_group, w_router, b_router, w_gate, w_up, w_down):
    p = _pack_params(norm1_g, w_in, conv_a_w, conv_a_b, ln_a_g, ln_a_b, conv_b_w, qn_c, kn_c, qn_d,
                     kn_d, w_out, norm2_g, w_group, b_group, w_router, b_router, w_gate, w_up, w_down)
    return _trunk((x_prompt, x_sample), p)
```
